```python
import math
import jax, jax.numpy as jnp
from jax import lax
import numpy as np

D_MODEL = 2048
BATCH = 1
SEQ = 16384
DEPTH = 4
DEC_BATCH = 16
DEC_SEQ = 64
PAST_LEN = 4096

CHUNK = 64
D_PLE = 256
D_MIX = D_MODEL
HEAD_DIM = 64
D_A = D_MIX // 4
H_B = (D_MIX - D_A) // (2 * HEAD_DIM)
H_C = H_B
D_B = H_B * HEAD_DIM
D_C = H_C * HEAD_DIM
K_CONV = 31
R_W = 64
R_A = 64
Q_BLOCK = 128
FORGET_BIAS = 4.0
RMS_EPS = 1e-6
LN_EPS = 1e-5
GN_EPS = 64e-5
SPLIT_SIZES = (D_A, D_A, D_A,
               D_B, D_B, D_B, D_B,
               D_C, D_C, D_C, D_C,
               H_C)
D_IN = 3 * D_A + 4 * D_B + 4 * D_C + H_C

kernel_name = 'hybrid_conv_rwkv7_fox_stream_step'


def rms_norm(x, g):
    xf = x.astype(jnp.float32)
    y = xf * lax.rsqrt(jnp.mean(xf * xf, -1, keepdims=True) + RMS_EPS)
    return (y * g.astype(jnp.float32)).astype(x.dtype)


def layer_norm(x, g, b):
    xf = x.astype(jnp.float32)
    mu = jnp.mean(xf, -1, keepdims=True)
    var = jnp.mean(jnp.square(xf - mu), -1, keepdims=True)
    y = (xf - mu) * lax.rsqrt(var + LN_EPS) * g.astype(jnp.float32) + b.astype(jnp.float32)
    return y.astype(x.dtype)


def conv_branch(u_val, u_glu, conv_state, w, b, ln_g, ln_b):
    u = u_val * jax.nn.sigmoid(u_glu)
    ext = jnp.concatenate([conv_state.astype(u.dtype), u], axis=1)
    y = lax.conv_general_dilated(ext, w[:, None, :].astype(ext.dtype), (1,), 'VALID',
                                 dimension_numbers=('NWC', 'WIO', 'NWC'),
                                 feature_group_count=D_A) + b.astype(ext.dtype)
    y = jax.nn.silu(layer_norm(y, ln_g, ln_b))
    return y, ext[:, -(K_CONV - 1):]


def rwkv7_branch(xn, r, k, v, s0, w0, w1, w2, a0, a1, a2, k_k, k_a, r_k, ln_g, ln_b):
    f32 = jnp.float32
    Bn, T, _ = r.shape
    heads = lambda t: t.astype(f32).reshape(Bn, T, H_B, HEAD_DIM)
    w_log = -jax.nn.softplus(-(w0 + jnp.tanh(xn @ w1) @ w2).astype(f32)) - 0.5
    decay = heads(jnp.exp(-jnp.exp(w_log)))
    a = heads(jax.nn.sigmoid((a0 + (xn @ a1) @ a2).astype(f32)))
    kk = heads(k * k_k)
    kk = kk / jnp.maximum(jnp.sqrt(jnp.sum(kk * kk, -1, keepdims=True)), 1e-12)
    kh = heads(k) * (1.0 + (a - 1.0) * k_a.astype(f32).reshape(H_B, HEAD_DIM))
    rh, vh = heads(r), heads(v)

    def step(S, inp):
        r_t, k_t, v_t, w_t, kk_t, a_t = inp
        sk = jnp.einsum('bhij,bhj->bhi', S, kk_t)
        S = (S * w_t[:, :, None, :] - sk[..., None] * (kk_t * a_t)[:, :, None, :]
             + v_t[..., None] * k_t[:, :, None, :])
        return S, jnp.einsum('bhij,bhj->bhi', S, r_t)

    xs = tuple(jnp.swapaxes(t, 0, 1) for t in (rh, kh, vh, decay, kk, a))
    s_T, ys = lax.scan(step, s0.astype(f32), xs)
    y = jnp.swapaxes(ys, 0, 1)
    mu = jnp.mean(y, -1, keepdims=True)
    var = jnp.mean(jnp.square(y - mu), -1, keepdims=True)
    y = ((y - mu) * lax.rsqrt(var + GN_EPS)).reshape(Bn, T, D_B) * ln_g.astype(f32) + ln_b.astype(f32)
    bonus = jnp.sum(rh * kh * r_k.astype(f32), -1, keepdims=True) * vh
    y = y + bonus.reshape(Bn, T, D_B)
    return y.astype(xn.dtype), s_T


def fox_attend(q, c_q, q_pos, k, v, c_k, k_pos):
    s = jnp.einsum('bqhd,bkhd->bhqk', q, k, preferred_element_type=jnp.float32) * (HEAD_DIM ** -0.5)
    s = s + jnp.swapaxes(c_q, 1, 2)[..., :, None] - jnp.swapaxes(c_k, 1, 2)[..., None, :]
    s = jnp.where(k_pos[None, :] <= q_pos[:, None], s, -jnp.inf)
    p = jax.nn.softmax(s, axis=-1)
    return jnp.einsum('bhqk,bkhd->bqhd', p.astype(v.dtype), v)


def fox_prompt(q, k, v, logf):
    Bn, T = q.shape[:2]
    c = jnp.cumsum(logf.astype(jnp.float32), axis=1)
    pos = jnp.arange(T)
    nb = T // Q_BLOCK
    qb = jnp.swapaxes(q.reshape(Bn, nb, Q_BLOCK, H_C, HEAD_DIM), 0, 1)
    cb = jnp.swapaxes(c.reshape(Bn, nb, Q_BLOCK, H_C), 0, 1)
    pb = pos.reshape(nb, Q_BLOCK)
    out = lax.map(lambda blk: fox_attend(blk[0], blk[1], blk[2], k, v, c, pos), (qb, cb, pb))
    return jnp.swapaxes(out, 0, 1).reshape(Bn, T, H_C, HEAD_DIM)


def fox_sample(q, k_new, v_new, logf_new, k_cache, v_cache, logf_cache):
    P = k_cache.shape[1]
    k = jnp.concatenate([k_cache.astype(k_new.dtype), k_new], axis=1)
    v = jnp.concatenate([v_cache.astype(v_new.dtype), v_new], axis=1)
    c = jnp.cumsum(jnp.concatenate([logf_cache.astype(jnp.float32), logf_new.astype(jnp.float32)], axis=1), axis=1)
    pos = jnp.arange(k.shape[1])
    return fox_attend(q, c[:, P:], pos[P:], k, v, c, pos)


def run_layer(h, p_l, conv_st, wkv_st, kv_cache, W):
    Bn, T, _ = h.shape
    xn = rms_norm(h, W['pre'])
    proj = xn @ W['w_in']
    offs, acc = [], 0
    for sz in SPLIT_SIZES[:-1]:
        acc += sz
        offs.append(acc)
    (a_val, a_glu, a_gate, r_b, k_b, v_b, g_b, q_c, k_c, v_c, g_c, f_c) = jnp.split(proj, offs, axis=-1)

    y_a, conv_new = conv_branch(a_val, a_glu, conv_st, W['conv_w'], W['conv_b'], W['conv_ln_g'], W['conv_ln_b'])
    y_a = y_a * jax.nn.silu(a_gate)

    y_b, wkv_new = rwkv7_branch(xn, r_b, k_b, v_b, wkv_st, W['w0'], W['w1'], W['w2'], W['a0'], W['a1'], W['a2'],
                                W['kk'], W['ka'], W['rk'], W['ln_g'], W['ln_b'])
    y_b = y_b * jax.nn.silu(g_b)

    heads = lambda t: t.reshape(Bn, T, H_C, HEAD_DIM)
    q, k, v = heads(q_c), heads(k_c), heads(v_c)
    logf = jax.nn.log_sigmoid(f_c.astype(jnp.float32) + W['b_f'].astype(jnp.float32))
    if kv_cache is None:
        o = fox_prompt(q, k, v, logf)
    else:
        o = fox_sample(q, k, v, logf, kv_cache[0], kv_cache[1], kv_cache[2])
    y_c = o.reshape(Bn, T, D_C).astype(h.dtype) * jax.nn.silu(g_c)

    mix = jnp.concatenate([y_a, y_b.astype(h.dtype), y_c], axis=-1) @ W['w_out']
    h = h + rms_norm(mix, W['post'])
    h = h + jax.nn.sigmoid(h @ W['ple_gate']) * (p_l @ W['ple_proj'])
    return h, conv_new, wkv_new, k, v, logf


def setup_inputs(seed: int = 0) -> dict:
    key = jax.random.key(seed)
    ks = iter(jax.random.split(key, 40))
    nrm = lambda shape, scale: scale * jax.random.normal(next(ks), shape, jnp.float32)
    return {
        'x_prompt': nrm((BATCH, SEQ, D_MODEL), 1.0),
        'x_sample': nrm((DEC_BATCH, DEC_SEQ, D_MODEL), 1.0),
        'p_prompt': nrm((DEPTH, BATCH, SEQ, D_PLE), 1.0),
        'p_sample': nrm((DEPTH, DEC_BATCH, DEC_SEQ, D_PLE), 1.0),
        'cache_k': nrm((DEPTH, DEC_BATCH, PAST_LEN, H_C, HEAD_DIM), 1.0),
        'cache_v': nrm((DEPTH, DEC_BATCH, PAST_LEN, H_C, HEAD_DIM), 1.0),
        'cache_logf': jax.nn.log_sigmoid(FORGET_BIAS + nrm((DEPTH, DEC_BATCH, PAST_LEN, H_C), 1.0)),
        'state_conv': nrm((DEPTH, DEC_BATCH, K_CONV - 1, D_A), 0.5),
        'state_wkv': nrm((DEPTH, DEC_BATCH, H_B, HEAD_DIM, HEAD_DIM), 0.3),
        'norm_pre_g': 1.0 + nrm((DEPTH, D_MODEL), 0.05),
        'norm_post_g': 1.0 + nrm((DEPTH, D_MODEL), 0.05),
        'w_in': nrm((DEPTH, D_MODEL, D_IN), D_MODEL ** -0.5),
        'b_f': FORGET_BIAS + nrm((DEPTH, H_C), 0.1),
        'conv_w': nrm((DEPTH, K_CONV, D_A), K_CONV ** -0.5),
        'conv_b': nrm((DEPTH, D_A), 0.02),
        'conv_ln_g': 1.0 + nrm((DEPTH, D_A), 0.05),
        'conv_ln_b': nrm((DEPTH, D_A), 0.02),
        'rwkv_w0': -2.0 + nrm((DEPTH, D_B), 0.5),
        'rwkv_w1': nrm((DEPTH, D_MODEL, R_W), D_MODEL ** -0.5),
        'rwkv_w2': nrm((DEPTH, R_W, D_B), 0.1 * R_W ** -0.5),
        'rwkv_a0': nrm((DEPTH, D_B), 0.1),
        'rwkv_a1': nrm((DEPTH, D_MODEL, R_A), D_MODEL ** -0.5),
        'rwkv_a2': nrm((DEPTH, R_A, D_B), 0.1 * R_A ** -0.5),
        'rwkv_kk': 0.85 + nrm((DEPTH, D_B), 0.05),
        'rwkv_ka': 1.0 + nrm((DEPTH, D_B), 0.05),
        'rwkv_rk': nrm((DEPTH, H_B, HEAD_DIM), 0.1),
        'rwkv_ln_g': 1.0 + nrm((DEPTH, D_B), 0.05),
        'rwkv_ln_b': nrm((DEPTH, D_B), 0.02),
        'w_out': nrm((DEPTH, D_MIX, D_MODEL), D_MIX ** -0.5),
        'ple_proj': nrm((DEPTH, D_PLE, D_MODEL), D_PLE ** -0.5),
        'ple_gate': nrm((DEPTH, D_MODEL, D_MODEL), D_MODEL ** -0.5),
    }


def reference(x_prompt, x_sample, p_prompt, p_sample, cache_k, cache_v, cache_logf, state_conv, state_wkv,
              norm_pre_g, norm_post_g, w_in, b_f, conv_w, conv_b, conv_ln_g, conv_ln_b,
              rwkv_w0, rwkv_w1, rwkv_w2, rwkv_a0, rwkv_a1, rwkv_a2, rwkv_kk, rwkv_ka, rwkv_rk,
              rwkv_ln_g, rwkv_ln_b, w_out, ple_proj, ple_gate):
    hp, hs = x_prompt, x_sample
    Bp = x_prompt.shape[0]
    zero_conv = jnp.zeros((Bp, K_CONV - 1, D_A), x_prompt.dtype)
    zero_wkv = jnp.zeros((Bp, H_B, HEAD_DIM, HEAD_DIM), jnp.float32)
    outs_p, outs_s = [], []
    for i in range(DEPTH):
        W = {'pre': norm_pre_g[i], 'post': norm_post_g[i], 'w_in': w_in[i], 'b_f': b_f[i],
             'conv_w': conv_w[i], 'conv_b': conv_b[i], 'conv_ln_g': conv_ln_g[i], 'conv_ln_b': conv_ln_b[i],
             'w0': rwkv_w0[i], 'w1': rwkv_w1[i], 'w2': rwkv_w2[i], 'a0': rwkv_a0[i], 'a1': rwkv_a1[i],
             'a2': rwkv_a2[i], 'kk': rwkv_kk[i], 'ka': rwkv_ka[i], 'rk': rwkv_rk[i],
             'ln_g': rwkv_ln_g[i], 'ln_b': rwkv_ln_b[i], 'w_out': w_out[i],
             'ple_proj': ple_proj[i], 'ple_gate': ple_gate[i]}
        hp, *st_p = run_layer(hp, p_prompt[i], zero_conv, zero_wkv, None, W)
        hs, *st_s = run_layer(hs, p_sample[i], state_conv[i], state_wkv[i],
                              (cache_k[i], cache_v[i], cache_logf[i]), W)
        outs_p.append(st_p)
        outs_s.append(st_s)
    stk = lambda outs, j: jnp.stack([o[j] for o in outs])
    conv_p, wkv_p, k_p, v_p, logf_p = stk(outs_p, 0), stk(outs_p, 1), stk(outs_p, 2), stk(outs_p, 3), stk(outs_p, 4)
    conv_s, wkv_s, k_s, v_s, logf_s = stk(outs_s, 0), stk(outs_s, 1), stk(outs_s, 2), stk(outs_s, 3), stk(outs_s, 4)
    return (hp, hs, conv_p, wkv_p, k_p, v_p, logf_p, conv_s, wkv_s, k_s, v_s, logf_s)
```

```python
import functools

import jax
import jax.numpy as jnp
from jax import lax
from jax.experimental import pallas as pl
from jax.experimental.pallas import tpu as pltpu

F32 = jnp.float32
BF16 = jnp.bfloat16

D_MODEL = 2048
D_PLE = 256
HEAD_DIM = 64
D_A = 512
N_HEADS = 12
D_B = N_HEADS * HEAD_DIM
D_C = N_HEADS * HEAD_DIM
K_CONV = 31
R_LORA = 64
RMS_EPS = 1e-6
LN_EPS = 1e-5
GN_EPS = 64e-5

LANES = 128
N_PAIR = N_HEADS // 2
CHUNK = 64
CONV_HALO = 32
CUM_W = 128

COL_A = 0
COL_B = 3 * D_A
COL_C = COL_B + 4 * D_B
COL_F = COL_C + 4 * D_C
COL_LORA = COL_F + LANES
N_EXT = 8192
VMEM_LIMIT = 52 * 1024 * 1024


def _tile(n, pref):
    t = min(pref, n)
    while n % t:
        t //= 2
    return t


def _dot(a, b):
    return jnp.dot(a, b, preferred_element_type=F32)


def _dot_nt(a, b):
    return lax.dot_general(a, b, (((1,), (1,)), ((), ())), preferred_element_type=F32)


def _dot_tn(a, b):
    return lax.dot_general(a, b, (((0,), (0,)), ((), ())), preferred_element_type=F32)


def _split3(x):
    hi = x.astype(BF16)
    r1 = x - hi.astype(F32)
    mid = r1.astype(BF16)
    lo = (r1 - mid.astype(F32)).astype(BF16)
    return hi, mid, lo


def _dot_exact_r(x, m):
    hi, mid, lo = _split3(x)
    return _dot(hi, m) + _dot(mid, m) + _dot(lo, m)


def _dot_exact_l(m, x):
    hi, mid, lo = _split3(x)
    return _dot(m, hi) + _dot(m, mid) + _dot(m, lo)


def _softplus(z):
    return jnp.maximum(z, 0.0) + jnp.log(1.0 + jnp.exp(-jnp.abs(z)))


def _silu(z):
    return z * jax.nn.sigmoid(z)


def _inproj_kernel(x_ref, g_ref, w_ref, o_ref):
    x = x_ref[...]
    ms = jnp.mean(x * x, axis=-1, keepdims=True)
    xn = (x * lax.rsqrt(ms + RMS_EPS)) * g_ref[...]
    o_ref[...] = _dot(xn.astype(BF16), w_ref[...])


def _inproj(h, g, w_ext):
    m, d = h.shape
    n = w_ext.shape[1]
    tm = _tile(m, 512)
    tn = 2048
    return pl.pallas_call(
        _inproj_kernel,
        grid=(n // tn, m // tm),
        in_specs=[pl.BlockSpec((tm, d), lambda j, i: (i, 0)),
                  pl.BlockSpec((1, d), lambda j, i: (0, 0)),
                  pl.BlockSpec((d, tn), lambda j, i: (0, j))],
        out_specs=pl.BlockSpec((tm, tn), lambda j, i: (i, j)),
        out_shape=jax.ShapeDtypeStruct((m, n), F32),
        compiler_params=pltpu.CompilerParams(
            dimension_semantics=("arbitrary", "arbitrary"), vmem_limit_bytes=VMEM_LIMIT),
    )(h, g, w_ext)


def _conv_kernel(av_ref, ag_ref, gate_ref, st_ref, w_ref, b_ref, lg_ref, lb_ref,
                 y_ref, sto_ref, ext_ref, *, tt, rb):
    t = pl.program_id(1)

    @pl.when(t == 0)
    def _init():
        ext_ref[0:8, :] = jnp.zeros((8, D_A), F32)
        ext_ref[CONV_HALO - (K_CONV - 1):CONV_HALO, :] = st_ref[0]

    ext_ref[CONV_HALO:CONV_HALO + tt, :] = av_ref[...] * jax.nn.sigmoid(ag_ref[...])
    off = CONV_HALO - (K_CONV - 1)

    for i in range(tt // rb):
        r0 = i * rb
        ys = []
        for c in range(D_A // LANES):
            cs = slice(LANES * c, LANES * (c + 1))
            acc = jnp.broadcast_to(b_ref[:, cs], (rb, LANES))
            for j in range(K_CONV):
                acc = acc + w_ref[j:j + 1, cs] * ext_ref[r0 + off + j:r0 + off + j + rb, cs]
            ys.append(acc)
        y = jnp.concatenate(ys, axis=1)
        mu = jnp.mean(y, axis=-1, keepdims=True)
        d = y - mu
        var = jnp.mean(d * d, axis=-1, keepdims=True)
        yn = d * lax.rsqrt(var + LN_EPS) * lg_ref[...] + lb_ref[...]
        out = _silu(yn) * _silu(gate_ref[r0:r0 + rb, :])
        y_ref[r0:r0 + rb, :] = out.astype(y_ref.dtype)
    sto_ref[0] = ext_ref[tt + off:tt + CONV_HALO, :]
    ext_ref[0:CONV_HALO, :] = ext_ref[tt:tt + CONV_HALO, :]


def _conv(proj, state, w, b, lg, lb, *, row0, nseq, t_seq):
    tt = _tile(t_seq, 256)
    rb = min(64, tt)
    nt = t_seq // tt
    blk0 = row0 // tt
    col = lambda c: pl.BlockSpec((tt, D_A), lambda s, t, c=c: (blk0 + s * nt + t, c))
    vec = lambda r: pl.BlockSpec((r, D_A), lambda s, t: (0, 0))
    y, st = pl.pallas_call(
        functools.partial(_conv_kernel, tt=tt, rb=rb),
        grid=(nseq, nt),
        in_specs=[col(0), col(1), col(2),
                  pl.BlockSpec((1, K_CONV - 1, D_A), lambda s, t: (s, 0, 0)),
                  vec(K_CONV), vec(1), vec(1), vec(1)],
        out_specs=[pl.BlockSpec((tt, D_A), lambda s, t: (s * nt + t, 0)),
                   pl.BlockSpec((1, K_CONV - 1, D_A), lambda s, t: (s, 0, 0))],
        out_shape=[jax.ShapeDtypeStruct((nseq * t_seq, D_A), BF16),
                   jax.ShapeDtypeStruct((nseq, K_CONV - 1, D_A), F32)],
        scratch_shapes=[pltpu.VMEM((tt + CONV_HALO, D_A), F32)],
        compiler_params=pltpu.CompilerParams(dimension_semantics=("arbitrary", "arbitrary")),
    )(proj, proj, proj, state, w, b, lg, lb)
    return y, st


def _rwkv_kernel(r_ref, k_ref, v_ref, g_ref, wa_ref, s0_ref, par_ref, w2_ref, a2_ref,
                 y_ref, so_ref, s_ref, *, n_prompt_chunks, pc_per_seq, sc_per_seq):
    i = pl.program_id(0)
    j = i - n_prompt_chunks
    first = jnp.where(i < n_prompt_chunks, i % pc_per_seq == 0, j % sc_per_seq == 0)

    @pl.when(first)
    def _init():
        s_ref[...] = s0_ref[0]

    n2 = 2 * CHUNK
    ri = lax.broadcasted_iota(jnp.int32, (n2, n2), 0)
    ci = lax.broadcasted_iota(jnp.int32, (n2, n2), 1)
    same = (ri >= CHUNK) == (ci >= CHUNK)
    m_strict = same & (ci < ri)
    m_incl = same & (ci <= ri)
    eye = (ri == ci).astype(F32)
    blk_ones = same.astype(BF16)
    r64 = lax.broadcasted_iota(jnp.int32, (CHUNK, CHUNK), 0)
    c64 = lax.broadcasted_iota(jnp.int32, (CHUNK, CHUNK), 1)
    tri = (c64 <= r64).astype(BF16)
    lm0 = lax.broadcasted_iota(jnp.int32, (CHUNK, LANES), 1) < HEAD_DIM

    def stack(z):
        return jnp.concatenate([jnp.where(lm0, z, 0.0), jnp.where(lm0, 0.0, z)], axis=0).astype(BF16)

    def dup(z):
        zb = z.astype(BF16)
        return jnp.concatenate([zb, zb], axis=0)

    wa = wa_ref[...]
    wa_t = jnp.tanh(wa).astype(BF16)
    wa_b = wa.astype(BF16)

    for p in range(N_PAIR):
        cs = slice(LANES * p, LANES * (p + 1))
        r = r_ref[:, cs]
        k = k_ref[:, cs]
        v = v_ref[:, cs]
        w0 = par_ref[0:1, cs]
        a0 = par_ref[1:2, cs]
        k_k = par_ref[2:3, cs]
        k_a = par_ref[3:4, cs]
        r_k = par_ref[4:5, cs]
        ln_g = par_ref[5:6, cs]
        ln_b = par_ref[6:7, cs]

        wl = w0 + _dot(wa_t, w2_ref[:, cs])
        lw = -jnp.exp(-_softplus(-wl) - 0.5)
        a = jax.nn.sigmoid(a0 + _dot(wa_b, a2_ref[:, cs]))
        kk = k * k_k
        ss = _dot_exact_r(kk * kk, blk_ones)
        kk = kk / jnp.maximum(jnp.sqrt(ss), 1e-12)
        kh = k * (1.0 + (a - 1.0) * k_a)
        bonus = _dot_exact_r(r * kh * r_k, blk_ones) * v

        cl = _dot_exact_l(tri, lw)
        cl_last = cl[CHUNK - 1:CHUNK, :]
        p_inv = jnp.exp(-cl)
        p_rem = jnp.exp(cl_last - cl)
        b = kk * a
        ls = stack(-(kk * jnp.exp(cl - lw)))
        rs = stack(r * jnp.exp(cl))
        vs = stack(v)
        kps = stack(kh * p_rem)
        bps = stack(b * p_rem)
        bb = dup(b * p_inv)
        kd = dup(kh * p_inv)

        x = jnp.where(m_strict, _dot_nt(ls, bb), 0.0)
        a_ak = jnp.where(m_strict, _dot_nt(ls, kd), 0.0).astype(BF16)
        a_rk = jnp.where(m_incl, _dot_nt(rs, kd), 0.0).astype(BF16)
        a_rb = jnp.where(m_incl, _dot_nt(rs, bb), 0.0).astype(BF16)

        tinv = eye + x
        xp = x
        for _ in range(5):
            xpb = xp.astype(BF16)
            xp = _dot(xpb, xpb)
            tinv = tinv + _dot(tinv.astype(BF16), xp.astype(BF16))

        s = s_ref[p]
        sb = s.astype(BF16)
        rhs = _dot_nt(ls, sb) + _dot(a_ak, vs)
        us = _dot(tinv.astype(BF16), rhs.astype(BF16)).astype(BF16)
        ys = _dot_nt(rs, sb) + _dot(a_rk, vs) + _dot(a_rb, us)
        y = ys[0:CHUNK] + ys[CHUNK:n2]
        s_ref[p] = s * jnp.exp(cl_last) + _dot_tn(vs, kps) + _dot_tn(us, bps)

        mu = _dot_exact_r(y, blk_ones) * (1.0 / HEAD_DIM)
        d = y - mu
        var = _dot_exact_r(d * d, blk_ones) * (1.0 / HEAD_DIM)
        yn = d * lax.rsqrt(var + GN_EPS) * ln_g + ln_b
        y_ref[:, cs] = ((yn + bonus) * _silu(g_ref[:, cs])).astype(y_ref.dtype)

    so_ref[0] = s_ref[...]


def _rwkv(proj, s0, par, w2e, a2e, *, n_prompt_seq, t_prompt, n_sample_seq, t_sample):
    pc = t_prompt // CHUNK
    sc = t_sample // CHUNK
    npc = n_prompt_seq * pc
    n_chunks = npc + n_sample_seq * sc
    seq_of = lambda i: jnp.where(i < npc, i // pc, n_prompt_seq + (i - npc) // sc)
    colb = lambda c: pl.BlockSpec((CHUNK, D_B), lambda i, c=c: (i, COL_B // D_B + c))
    full = lambda a: pl.BlockSpec(a.shape, lambda i: (0,) * a.ndim)
    y, so = pl.pallas_call(
        functools.partial(_rwkv_kernel, n_prompt_chunks=npc, pc_per_seq=pc, sc_per_seq=sc),
        grid=(n_chunks,),
        in_specs=[colb(0), colb(1), colb(2), colb(3),
                  pl.BlockSpec((CHUNK, LANES), lambda i: (i, COL_LORA // LANES)),
                  pl.BlockSpec((1, N_PAIR, LANES, LANES), lambda i: (seq_of(i), 0, 0, 0)),
                  full(par), full(w2e), full(a2e)],
        out_specs=[pl.BlockSpec((CHUNK, D_B), lambda i: (i, 0)),
                   pl.BlockSpec((1, N_PAIR, LANES, LANES), lambda i: (seq_of(i), 0, 0, 0))],
        out_shape=[jax.ShapeDtypeStruct((n_chunks * CHUNK, D_B), BF16),
                   jax.ShapeDtypeStruct(s0.shape, F32)],
        scratch_shapes=[pltpu.VMEM((N_PAIR, LANES, LANES), F32)],
        compiler_params=pltpu.CompilerParams(dimension_semantics=("arbitrary",)),
    )(proj, proj, proj, proj, proj, s0, par, w2e, a2e)
    return y, so


def _pair_blockdiag(s):
    b = s.shape[0]
    s = s.reshape(b, N_PAIR, 2, HEAD_DIM, HEAD_DIM)
    z = jnp.zeros_like(s[:, :, 0])
    top = jnp.concatenate([s[:, :, 0], z], axis=-1)
    bot = jnp.concatenate([z, s[:, :, 1]], axis=-1)
    return jnp.concatenate([top, bot], axis=-2)


def _pair_unblock(sp):
    b = sp.shape[0]
    h0 = sp[:, :, :HEAD_DIM, :HEAD_DIM]
    h1 = sp[:, :, HEAD_DIM:, HEAD_DIM:]
    return jnp.stack([h0, h1], axis=2).reshape(b, N_HEADS, HEAD_DIM, HEAD_DIM)


def _fox_prep_kernel(x_ref, b_ref, lf_ref, c_ref, carry_ref, *, n_chunks, n_raw):
    rr = lax.broadcasted_iota(jnp.int32, (CUM_W, CUM_W), 0)
    cc = lax.broadcasted_iota(jnp.int32, (CUM_W, CUM_W), 1)
    upper = (rr <= cc).astype(BF16)
    ones = jnp.ones((CUM_W, CUM_W), BF16)
    carry_ref[...] = jnp.zeros_like(carry_ref)

    def step(ci, x):
        lf_ref[0, ci] = x
        c_ref[0, ci] = _dot_exact_r(x, upper) + carry_ref[...]
        carry_ref[...] = carry_ref[...] + _dot_exact_r(x, ones)

    def body_keep(ci, carry):
        step(ci, x_ref[0, ci])
        return carry

    def body_raw(ci, carry):
        step(ci, -_softplus(-(x_ref[0, ci] + b_ref[...])))
        return carry

    lax.fori_loop(0, n_chunks - n_raw, body_keep, 0)
    lax.fori_loop(n_chunks - n_raw, n_chunks, body_raw, 0)


def _fox_prep(x, b, n_raw):
    nb, nch = x.shape[:2]
    spec = pl.BlockSpec((1,) + x.shape[1:], lambda s: (s, 0, 0, 0))
    return pl.pallas_call(
        functools.partial(_fox_prep_kernel, n_chunks=nch, n_raw=n_raw),
        grid=(nb,),
        in_specs=[spec, pl.BlockSpec(b.shape, lambda s: (0, 0))],
        out_specs=[spec, spec],
        out_shape=[jax.ShapeDtypeStruct(x.shape, F32)] * 2,
        scratch_shapes=[pltpu.VMEM(x.shape[2:], F32)],
        compiler_params=pltpu.CompilerParams(dimension_semantics=("arbitrary",)),
    )(x, b)


def _to_time_chunks(x):
    b, t, hh = x.shape
    x = jnp.pad(x, ((0, 0), (0, 0), (0, 16 - hh)))
    return x.reshape(b, t // CUM_W, CUM_W, 16).transpose(0, 1, 3, 2)


def _from_time_chunks(x):
    b, nch = x.shape[:2]
    return x.transpose(0, 1, 3, 2).reshape(b, nch * CUM_W, 16)[:, :, :N_HEADS]


NEG = -1e30


def _attn_prompt_kernel(q_ref, k_ref, v_ref, c_ref, g_ref, o_ref, m_ref, l_ref, acc_ref, *, bq):
    i = pl.program_id(1)
    q = q_ref[...]
    lm0 = lax.broadcasted_iota(jnp.int32, (bq, LANES), 1) < HEAD_DIM
    row = lax.broadcasted_iota(jnp.int32, (bq, bq), 0)
    col = lax.broadcasted_iota(jnp.int32, (bq, bq), 1)
    causal = col <= row
    scale = jnp.asarray(HEAD_DIM ** -0.5, BF16)
    outs = []
    for h in range(2):
        qh = jnp.where(lm0 if h == 0 else jnp.logical_not(lm0), q, jnp.zeros_like(q)) * scale
        m_ref[...] = jnp.full(m_ref.shape, NEG, F32)
        l_ref[...] = jnp.zeros(l_ref.shape, F32)
        acc_ref[...] = jnp.zeros(acc_ref.shape, F32)

        def block(jb, mask):
            r0 = pl.multiple_of(jb * bq, bq)
            s = _dot_nt(qh, k_ref[pl.ds(r0, bq), :]) - c_ref[0, jb, h:h + 1, :]
            if mask:
                s = jnp.where(causal, s, NEG)
            m_old = m_ref[...]
            m_new = jnp.maximum(m_old, jnp.max(s, axis=1, keepdims=True))
            alpha = jnp.exp(m_old - m_new)
            pr = jnp.exp(s - m_new)
            l_ref[...] = alpha * l_ref[...] + jnp.sum(pr, axis=1, keepdims=True)
            acc_ref[...] = alpha * acc_ref[...] + _dot(pr.astype(BF16), v_ref[pl.ds(r0, bq), :])
            m_ref[...] = m_new

        def body(jb, carry):
            block(jb, False)
            return carry

        lax.fori_loop(0, i, body, 0)
        block(i, True)
        outs.append(acc_ref[...] / l_ref[...])
    o = jnp.where(lm0, outs[0], outs[1]) * _silu(g_ref[...])
    o_ref[...] = o.astype(o_ref.dtype)


def _attn_prompt(qkv, c, proj, *, t):
    bq = c.shape[-1]
    nq = t // bq
    return pl.pallas_call(
        functools.partial(_attn_prompt_kernel, bq=bq),
        grid=(N_PAIR, nq),
        in_specs=[pl.BlockSpec((bq, LANES), lambda p, i: (i, p)),
                  pl.BlockSpec((t, LANES), lambda p, i: (0, N_PAIR + p)),
                  pl.BlockSpec((t, LANES), lambda p, i: (0, 2 * N_PAIR + p)),
                  pl.BlockSpec((1, nq, 2, bq), lambda p, i: (p, 0, 0, 0)),
                  pl.BlockSpec((bq, LANES), lambda p, i: (i, (COL_C + 3 * D_C) // LANES + p))],
        out_specs=pl.BlockSpec((bq, LANES), lambda p, i: (i, p)),
        out_shape=jax.ShapeDtypeStruct((t, D_C), BF16),
        scratch_shapes=[pltpu.VMEM((bq, 1), F32), pltpu.VMEM((bq, 1), F32), pltpu.VMEM((bq, LANES), F32)],
        compiler_params=pltpu.CompilerParams(
            dimension_semantics=("arbitrary", "arbitrary"), vmem_limit_bytes=VMEM_LIMIT),
    )(qkv, qkv, qkv, c, proj)


def _attn_sample_kernel(q_ref, kn_ref, vn_ref, g_ref, kc_ref, vc_ref, c_ref, o_ref, *, past, ts):
    q = q_ref[...].astype(BF16)
    lm0 = lax.broadcasted_iota(jnp.int32, (ts, LANES), 1) < HEAD_DIM
    row = lax.broadcasted_iota(jnp.int32, (ts, ts), 0)
    col = lax.broadcasted_iota(jnp.int32, (ts, ts), 1)
    scale = jnp.asarray(HEAD_DIM ** -0.5, BF16)
    kc = kc_ref[0].astype(BF16)
    vc = vc_ref[0].astype(BF16)
    kn = kn_ref[...].astype(BF16)
    vn = vn_ref[...].astype(BF16)
    outs = []
    for h in range(2):
        qh = jnp.where(lm0 if h == 0 else jnp.logical_not(lm0), q, jnp.zeros_like(q)) * scale
        s1 = _dot_nt(qh, kc) - c_ref[0, 0, h:h + 1, 0:past]
        s2 = _dot_nt(qh, kn) - c_ref[0, 0, h:h + 1, past:past + ts]
        s2 = jnp.where(col <= row, s2, NEG)
        m = jnp.maximum(jnp.max(s1, axis=1, keepdims=True), jnp.max(s2, axis=1, keepdims=True))
        p1 = jnp.exp(s1 - m)
        p2 = jnp.exp(s2 - m)
        l = jnp.sum(p1, axis=1, keepdims=True) + jnp.sum(p2, axis=1, keepdims=True)
        outs.append((_dot(p1.astype(BF16), vc) + _dot(p2.astype(BF16), vn)) / l)
    o = jnp.where(lm0, outs[0], outs[1]) * _silu(g_ref[...])
    o_ref[...] = o.astype(o_ref.dtype)


def _attn_sample(proj, cache_k, cache_v, c, *, row0, nseq, ts):
    past = cache_k.shape[1]
    blk0 = row0 // ts
    pcol = lambda base: pl.BlockSpec((ts, LANES), lambda s, p, base=base: (blk0 + s, base // LANES + p))
    cache = pl.BlockSpec((1, past, LANES), lambda s, p: (s, 0, p))
    return pl.pallas_call(
        functools.partial(_attn_sample_kernel, past=past, ts=ts),
        grid=(nseq, N_PAIR),
        in_specs=[pcol(COL_C), pcol(COL_C + D_C), pcol(COL_C + 2 * D_C), pcol(COL_C + 3 * D_C),
                  cache, cache,
                  pl.BlockSpec((1, 1, 2, c.shape[-1]), lambda s, p: (s, p, 0, 0))],
        out_specs=pl.BlockSpec((ts, LANES), lambda s, p: (s, p)),
        out_shape=jax.ShapeDtypeStruct((nseq * ts, D_C), BF16),
        compiler_params=pltpu.CompilerParams(
            dimension_semantics=("arbitrary", "arbitrary"), vmem_limit_bytes=VMEM_LIMIT),
    )(proj, proj, proj, proj, cache_k, cache_v, c)


def _outproj_kernel(ya_ref, yb_ref, yc_ref, h_ref, w_ref, g_ref, o_ref):
    mix = (_dot(ya_ref[...], w_ref[0:D_A, :])
           + _dot(yb_ref[...], w_ref[D_A:D_A + D_B, :])
           + _dot(yc_ref[...], w_ref[D_A + D_B:, :]))
    ms = jnp.mean(mix * mix, axis=-1, keepdims=True)
    o_ref[...] = h_ref[...] + (mix * lax.rsqrt(ms + RMS_EPS)) * g_ref[...]


def _outproj(ya, yb, yc, h, w_out, g):
    m, d = h.shape
    tm = _tile(m, 512)
    rows = lambda w: pl.BlockSpec((tm, w), lambda i: (i, 0))
    return pl.pallas_call(
        _outproj_kernel,
        grid=(m // tm,),
        in_specs=[rows(D_A), rows(D_B), rows(D_C), rows(d),
                  pl.BlockSpec(w_out.shape, lambda i: (0, 0)),
                  pl.BlockSpec((1, d), lambda i: (0, 0))],
        out_specs=rows(d),
        out_shape=jax.ShapeDtypeStruct((m, d), F32),
        compiler_params=pltpu.CompilerParams(
            dimension_semantics=("arbitrary",), vmem_limit_bytes=VMEM_LIMIT),
    )(ya, yb, yc, h, w_out, g)


def _ple_kernel(h_ref, p_ref, wg_ref, wp_ref, o_ref):
    h = h_ref[...]
    gate = jax.nn.sigmoid(_dot(h.astype(BF16), wg_ref[...]))
    o_ref[...] = h + gate * _dot(p_ref[...].astype(BF16), wp_ref[...])


def _ple(h, p, w_gate, w_proj):
    m, d = h.shape
    tm = _tile(m, 512)
    return pl.pallas_call(
        _ple_kernel,
        grid=(m // tm,),
        in_specs=[pl.BlockSpec((tm, d), lambda i: (i, 0)),
                  pl.BlockSpec((tm, D_PLE), lambda i: (i, 0)),
                  pl.BlockSpec(w_gate.shape, lambda i: (0, 0)),
                  pl.BlockSpec(w_proj.shape, lambda i: (0, 0))],
        out_specs=pl.BlockSpec((tm, d), lambda i: (i, 0)),
        out_shape=jax.ShapeDtypeStruct((m, d), F32),
        compiler_params=pltpu.CompilerParams(
            dimension_semantics=("arbitrary",), vmem_limit_bytes=VMEM_LIMIT),
    )(h, p, w_gate, w_proj)


def kernel(x_prompt, x_sample, p_prompt, p_sample, cache_k, cache_v, cache_logf, state_conv, state_wkv, norm_pre_g, norm_post_g, w_in, b_f, conv_w, conv_b, conv_ln_g, conv_ln_b, rwkv_w0, rwkv_w1, rwkv_w2, rwkv_a0, rwkv_a1, rwkv_a2, rwkv_kk, rwkv_ka, rwkv_rk, rwkv_ln_g, rwkv_ln_b, w_out, ple_proj, ple_gate):
    depth = w_in.shape[0]
    bp, tp, d = x_prompt.shape
    bs, ts, _ = x_sample.shape
    past = cache_k.shape[2]
    n_p = bp * tp
    n_s = bs * ts
    assert tp % CHUNK == 0 and ts == CHUNK and past % CUM_W == 0 and tp % CUM_W == 0 and bp == 1
    bq = _tile(tp, 256)

    zpad = lambda n: jnp.zeros((depth, d, n), F32)
    w_ext = jnp.concatenate(
        [w_in, zpad(COL_LORA - w_in.shape[2]), rwkv_w1, rwkv_a1, zpad(N_EXT - COL_LORA - 2 * R_LORA)],
        axis=2).astype(BF16)
    zl = jnp.zeros((depth, R_LORA, D_B), F32)
    w2e = jnp.concatenate([rwkv_w2, zl], axis=1).astype(BF16)
    a2e = jnp.concatenate([zl, rwkv_a2], axis=1).astype(BF16)
    par = jnp.stack([rwkv_w0, rwkv_a0, rwkv_kk, rwkv_ka, rwkv_rk.reshape(depth, D_B), rwkv_ln_g, rwkv_ln_b,
                     jnp.zeros_like(rwkv_w0)], axis=1)
    w_out_b = w_out.astype(BF16)
    ple_gate_b = ple_gate.astype(BF16)
    ple_proj_b = ple_proj.astype(BF16)
    b_f16 = jnp.broadcast_to(jnp.pad(b_f, ((0, 0), (0, 16 - N_HEADS)))[:, :, None], (depth, 16, CUM_W))

    h = jnp.concatenate([x_prompt.reshape(n_p, d), x_sample.reshape(n_s, d)], axis=0)
    p_all = jnp.concatenate([p_prompt.reshape(depth, n_p, D_PLE), p_sample.reshape(depth, n_s, D_PLE)], axis=1)
    zero_conv = jnp.zeros((bp, K_CONV - 1, D_A), F32)
    zero_wkv = jnp.zeros((bp, N_HEADS, HEAD_DIM, HEAD_DIM), F32)

    outs = {name: [] for name in ("conv_p", "wkv_p", "k_p", "v_p", "lf_p", "conv_s", "wkv_s", "k_s", "v_s", "lf_s")}
    for l in range(depth):
        proj = _inproj(h, norm_pre_g[l][None], w_ext[l])

        ya_p, conv_p = _conv(proj, zero_conv, conv_w[l], conv_b[l][None], conv_ln_g[l][None],
                             conv_ln_b[l][None], row0=0, nseq=bp, t_seq=tp)
        ya_s, conv_s = _conv(proj, state_conv[l], conv_w[l], conv_b[l][None], conv_ln_g[l][None],
                             conv_ln_b[l][None], row0=n_p, nseq=bs, t_seq=ts)
        ya = jnp.concatenate([ya_p, ya_s], axis=0)

        s0 = _pair_blockdiag(jnp.concatenate([zero_wkv, state_wkv[l]], axis=0))
        yb, s_new = _rwkv(proj, s0, par[l], w2e[l], a2e[l],
                          n_prompt_seq=bp, t_prompt=tp, n_sample_seq=bs, t_sample=ts)
        wkv = _pair_unblock(s_new)

        k_new = proj[:, COL_C + D_C:COL_C + 2 * D_C]
        v_new = proj[:, COL_C + 2 * D_C:COL_C + 3 * D_C]
        f_raw = proj[:, COL_F:COL_F + N_HEADS]

        lf_pc, c_pc = _fox_prep(_to_time_chunks(f_raw[:n_p].reshape(bp, tp, N_HEADS)), b_f16[l], tp // CUM_W)
        lf_p = _from_time_chunks(lf_pc)
        c_p = c_pc[0].transpose(1, 0, 2).reshape(16, tp)[:N_HEADS]
        c_p = c_p.reshape(N_PAIR, 2, tp // bq, bq).transpose(0, 2, 1, 3)
        qkv_p = proj[:n_p, COL_C:COL_C + 3 * D_C].astype(BF16)
        yc_p = _attn_prompt(qkv_p, c_p, proj, t=tp)

        x_s = jnp.concatenate(
            [_to_time_chunks(cache_logf[l]),
             _to_time_chunks(jnp.pad(f_raw[n_p:].reshape(bs, ts, N_HEADS), ((0, 0), (0, CUM_W - ts), (0, 0))))],
            axis=1)
        lf_sc, c_sc = _fox_prep(x_s, b_f16[l], 1)
        lf_s = _from_time_chunks(lf_sc[:, -1:])[:, :ts]
        c_s = c_sc.transpose(0, 2, 1, 3).reshape(bs, 16, past + CUM_W)[:, :N_HEADS]
        c_s = c_s.reshape(bs, N_PAIR, 2, past + CUM_W)
        yc_s = _attn_sample(proj, cache_k[l].reshape(bs, past, D_C), cache_v[l].reshape(bs, past, D_C), c_s,
                            row0=n_p, nseq=bs, ts=ts)
        yc = jnp.concatenate([yc_p, yc_s], axis=0)

        h = _outproj(ya, yb, yc, h, w_out_b[l], norm_post_g[l][None])
        h = _ple(h, p_all[l], ple_gate_b[l], ple_proj_b[l])

        outs["conv_p"].append(conv_p)
        outs["conv_s"].append(conv_s)
        outs["wkv_p"].append(wkv[:bp])
        outs["wkv_s"].append(wkv[bp:])
        outs["k_p"].append(k_new[:n_p].reshape(bp, tp, N_HEADS, HEAD_DIM))
        outs["v_p"].append(v_new[:n_p].reshape(bp, tp, N_HEADS, HEAD_DIM))
        outs["k_s"].append(k_new[n_p:].reshape(bs, ts, N_HEADS, HEAD_DIM))
        outs["v_s"].append(v_new[n_p:].reshape(bs, ts, N_HEADS, HEAD_DIM))
        outs["lf_p"].append(lf_p)
        outs["lf_s"].append(lf_s)

    st = lambda name: jnp.stack(outs[name])
    return (h[:n_p].reshape(bp, tp, d), h[n_p:].reshape(bs, ts, d),
            st("conv_p"), st("wkv_p"), st("k_p"), st("v_p"), st("lf_p"),
            st("conv_s"), st("wkv_s"), st("k_s"), st("v_s"), st("lf_s"))
```

```python
import functools

import jax
import jax.numpy as jnp
from jax import lax
from jax.experimental import pallas as pl
from jax.experimental.pallas import tpu as pltpu

F32 = jnp.float32
BF16 = jnp.bfloat16

D_MODEL = 2048
D_PLE = 256
HEAD_DIM = 64
D_A = 512
N_HEADS = 12
D_B = N_HEADS * HEAD_DIM
D_C = N_HEADS * HEAD_DIM
K_CONV = 31
R_LORA = 64
RMS_EPS = 1e-6
LN_EPS = 1e-5
GN_EPS = 64e-5

LANES = 128
N_PAIR = N_HEADS // 2
CHUNK = 64
CONV_HALO = 32
CUM_W = 128
ATTN_BLOCK = 512

COL_A = 0
COL_B = 3 * D_A
COL_C = COL_B + 4 * D_B
COL_F = COL_C + 4 * D_C
COL_LORA = COL_F + LANES
N_EXT = 8192
VMEM_LIMIT = 52 * 1024 * 1024


def _tile(n, pref):
    t = min(pref, n)
    while n % t:
        t //= 2
    return t


def _dot(a, b):
    return jnp.dot(a, b, preferred_element_type=F32)


def _dot_nt(a, b):
    return lax.dot_general(a, b, (((1,), (1,)), ((), ())), preferred_element_type=F32)


def _dot_tn(a, b):
    return lax.dot_general(a, b, (((0,), (0,)), ((), ())), preferred_element_type=F32)


def _split3(x):
    hi = x.astype(BF16)
    r1 = x - hi.astype(F32)
    mid = r1.astype(BF16)
    lo = (r1 - mid.astype(F32)).astype(BF16)
    return hi, mid, lo


def _dot_exact_r(x, m):
    hi, mid, lo = _split3(x)
    return _dot(hi, m) + _dot(mid, m) + _dot(lo, m)


def _dot_split_r(x, m):
    hi = x.astype(BF16)
    lo = (x - hi.astype(F32)).astype(BF16)
    return _dot(hi, m) + _dot(lo, m)


def _dot_split_l(m, x):
    hi = x.astype(BF16)
    lo = (x - hi.astype(F32)).astype(BF16)
    return _dot(m, hi) + _dot(m, lo)


def _softplus(z):
    return jnp.maximum(z, 0.0) + jnp.log(1.0 + jnp.exp(-jnp.abs(z)))


def _silu(z):
    return z * jax.nn.sigmoid(z)


def _inproj_kernel(x_ref, g_ref, w_ref, o_ref):
    x = x_ref[...]
    ms = jnp.mean(x * x, axis=-1, keepdims=True)
    xn = (x * lax.rsqrt(ms + RMS_EPS)) * g_ref[...]
    o_ref[...] = _dot(xn.astype(BF16), w_ref[...])


def _inproj(h, g, w_ext):
    m, d = h.shape
    n = w_ext.shape[1]
    tm = _tile(m, 512)
    tn = 2048
    return pl.pallas_call(
        _inproj_kernel,
        grid=(n // tn, m // tm),
        in_specs=[pl.BlockSpec((tm, d), lambda j, i: (i, 0)),
                  pl.BlockSpec((1, d), lambda j, i: (0, 0)),
                  pl.BlockSpec((d, tn), lambda j, i: (0, j))],
        out_specs=pl.BlockSpec((tm, tn), lambda j, i: (i, j)),
        out_shape=jax.ShapeDtypeStruct((m, n), F32),
        compiler_params=pltpu.CompilerParams(
            dimension_semantics=("arbitrary", "arbitrary"), vmem_limit_bytes=VMEM_LIMIT),
        name="inproj",
    )(h, g, w_ext)


def _conv_kernel(av_ref, ag_ref, gate_ref, st_ref, w_ref, b_ref, lg_ref, lb_ref,
                 y_ref, sto_ref, ext_ref, *, tt, rb):
    t = pl.program_id(1)

    @pl.when(t == 0)
    def _init():
        ext_ref[0:8, :] = jnp.zeros((8, D_A), F32)
        ext_ref[CONV_HALO - (K_CONV - 1):CONV_HALO, :] = st_ref[0]

    ext_ref[CONV_HALO:CONV_HALO + tt, :] = av_ref[...] * jax.nn.sigmoid(ag_ref[...])
    off = CONV_HALO - (K_CONV - 1)

    for i in range(tt // rb):
        r0 = i * rb
        ys = []
        for c in range(D_A // LANES):
            cs = slice(LANES * c, LANES * (c + 1))
            acc = jnp.broadcast_to(b_ref[:, cs], (rb, LANES))
            for j in range(K_CONV):
                acc = acc + w_ref[j:j + 1, cs] * ext_ref[r0 + off + j:r0 + off + j + rb, cs]
            ys.append(acc)
        y = jnp.concatenate(ys, axis=1)
        mu = jnp.mean(y, axis=-1, keepdims=True)
        d = y - mu
        var = jnp.mean(d * d, axis=-1, keepdims=True)
        yn = d * lax.rsqrt(var + LN_EPS) * lg_ref[...] + lb_ref[...]
        out = _silu(yn) * _silu(gate_ref[r0:r0 + rb, :])
        y_ref[r0:r0 + rb, :] = out.astype(y_ref.dtype)
    sto_ref[0] = ext_ref[tt + off:tt + CONV_HALO, :]
    ext_ref[0:CONV_HALO, :] = ext_ref[tt:tt + CONV_HALO, :]


def _conv(proj, state, w, b, lg, lb, *, row0, nseq, t_seq):
    tt = _tile(t_seq, 256)
    rb = min(64, tt)
    nt = t_seq // tt
    blk0 = row0 // tt
    col = lambda c: pl.BlockSpec((tt, D_A), lambda s, t, c=c: (blk0 + s * nt + t, c))
    vec = lambda r: pl.BlockSpec((r, D_A), lambda s, t: (0, 0))
    y, st = pl.pallas_call(
        functools.partial(_conv_kernel, tt=tt, rb=rb),
        grid=(nseq, nt),
        in_specs=[col(0), col(1), col(2),
                  pl.BlockSpec((1, K_CONV - 1, D_A), lambda s, t: (s, 0, 0)),
                  vec(K_CONV), vec(1), vec(1), vec(1)],
        out_specs=[pl.BlockSpec((tt, D_A), lambda s, t: (s * nt + t, 0)),
                   pl.BlockSpec((1, K_CONV - 1, D_A), lambda s, t: (s, 0, 0))],
        out_shape=[jax.ShapeDtypeStruct((nseq * t_seq, D_A), BF16),
                   jax.ShapeDtypeStruct((nseq, K_CONV - 1, D_A), F32)],
        scratch_shapes=[pltpu.VMEM((tt + CONV_HALO, D_A), F32)],
        compiler_params=pltpu.CompilerParams(dimension_semantics=("arbitrary", "arbitrary")),
        name="conv_branch",
    )(proj, proj, proj, state, w, b, lg, lb)
    return y, st


def _rwkv_kernel(r_ref, k_ref, v_ref, g_ref, wa_ref, s0_ref, par_ref, wl_ref,
                 y_ref, so_ref, *scratch, nc, chunk_is_seq):
    rows = nc * CHUNK
    n2 = 2 * CHUNK
    dp = 2 * LANES
    ri = lax.broadcasted_iota(jnp.int32, (n2, n2), 0)
    ci = lax.broadcasted_iota(jnp.int32, (n2, n2), 1)
    same = (ri >= CHUNK) == (ci >= CHUNK)
    m_strict = same & (ci < ri)
    m_incl = same & (ci <= ri)
    eye = (ri == ci).astype(F32)
    rt = lax.broadcasted_iota(jnp.int32, (rows, rows), 0)
    ct = lax.broadcasted_iota(jnp.int32, (rows, rows), 1)
    tri = (((rt // CHUNK) == (ct // CHUNK)) & (ct <= rt)).astype(BF16)
    rh = lax.broadcasted_iota(jnp.int32, (dp, dp), 0)
    ch = lax.broadcasted_iota(jnp.int32, (dp, dp), 1)
    head_ones = ((rh // HEAD_DIM) == (ch // HEAD_DIM)).astype(BF16)
    lm0 = lax.broadcasted_iota(jnp.int32, (CHUNK, LANES), 1) < HEAD_DIM

    def stack(z):
        return jnp.concatenate([jnp.where(lm0, z, 0.0), jnp.where(lm0, 0.0, z)], axis=0).astype(BF16)

    def dup(z):
        zb = z.astype(BF16)
        return jnp.concatenate([zb, zb], axis=0)

    if not chunk_is_seq:
        s_ref = scratch[0]

        @pl.when(pl.program_id(0) == 0)
        def _init():
            s_ref[...] = s0_ref[0]

    wa = wa_ref[...]
    lane = lax.broadcasted_iota(jnp.int32, wa.shape, 1)
    lora = _dot(jnp.where(lane < R_LORA, jnp.tanh(wa), wa).astype(BF16), wl_ref[...])
    lw_all = -jnp.exp(-_softplus(-(par_ref[0:1, :] + lora[:, :D_B])) - 0.5)
    a_all = jax.nn.sigmoid(par_ref[1:2, :] + lora[:, D_B:])
    cl_all = _dot_split_l(tri, lw_all)

    chains = {}
    bonus_q = []
    for q in range(N_PAIR // 2):
        qs = slice(dp * q, dp * (q + 1))
        r = r_ref[:, qs]
        k = k_ref[:, qs]
        v = v_ref[:, qs]
        a = a_all[:, qs]
        lw = lw_all[:, qs]
        cl = cl_all[:, qs]
        kk = k * par_ref[2:3, qs]
        kh = k * (1.0 + (a - 1.0) * par_ref[3:4, qs])
        sums = _dot_split_r(jnp.concatenate([kk * kk, r * kh * par_ref[4:5, qs]], axis=0), head_ones)
        kk = kk / jnp.maximum(jnp.sqrt(sums[:rows]), 1e-12)
        bonus = sums[rows:] * v
        b = kk * a
        at = -(kk * jnp.exp(cl - lw))
        rt_ = r * jnp.exp(cl)
        p_inv = jnp.exp(-cl)
        kt = kh * p_inv
        bt = b * p_inv
        cl_last = jnp.concatenate(
            [jnp.broadcast_to(cl[CHUNK * (c + 1) - 1:CHUNK * (c + 1), :], (CHUNK, dp)) for c in range(nc)], axis=0)
        p_rem = jnp.exp(cl_last - cl)
        kp = kh * p_rem
        bp = b * p_rem
        p_end = jnp.exp(cl_last)

        bonus_q.append(bonus)
        for hh in range(2):
            ls_ = slice(LANES * hh, LANES * (hh + 1))
            for c in range(nc):
                rs_ = slice(CHUNK * c, CHUNK * (c + 1))
                ch_ = dict(
                    lr=jnp.concatenate([stack(at[rs_, ls_]), stack(rt_[rs_, ls_])], axis=0),
                    bk=jnp.concatenate([dup(bt[rs_, ls_]), dup(kt[rs_, ls_])], axis=0),
                    vs=stack(v[rs_, ls_]),
                    kbp=jnp.concatenate([stack(kp[rs_, ls_]), stack(bp[rs_, ls_])], axis=0),
                    pe=p_end[CHUNK * c:CHUNK * c + 1, ls_])
                chains[(2 * q + hh, c)] = ch_

    order = [(p, c) for c in range(nc) for p in range(N_PAIR)]
    for key in order:
        ch_ = chains[key]
        gm = _dot_nt(ch_["lr"], ch_.pop("bk"))
        x = jnp.where(m_strict, gm[:n2, :n2], 0.0)
        ch_["a_k"] = jnp.concatenate([jnp.where(m_strict, gm[:n2, n2:], 0.0),
                                      jnp.where(m_incl, gm[n2:, n2:], 0.0)], axis=0).astype(BF16)
        ch_["a_rb"] = jnp.where(m_incl, gm[n2:, :n2], 0.0).astype(BF16)
        ch_["tinv"] = eye + x
        ch_["xb"] = x.astype(BF16)
    for key in order:
        ch_ = chains[key]
        xb = ch_.pop("xb")
        ch_["xx"] = _dot(xb, xb).astype(BF16)
        ch_["av"] = _dot(ch_.pop("a_k"), ch_["vs"])
    for _ in range(4):
        for key in order:
            ch_ = chains[key]
            both = _dot(ch_["xx"], jnp.concatenate([ch_["tinv"].astype(BF16), ch_["xx"]], axis=1))
            ch_["tinv"] = ch_["tinv"] + both[:, :n2]
            ch_["xx"] = both[:, n2:].astype(BF16)
    for key in order:
        ch_ = chains[key]
        ch_["tinv"] = (ch_["tinv"] + _dot(ch_.pop("xx"), ch_["tinv"].astype(BF16))).astype(BF16)

    states = [None if chunk_is_seq else s_ref[p] for p in range(N_PAIR)]
    ys = {}
    for c in range(nc):
        if chunk_is_seq:
            states = [s0_ref[c, p] for p in range(N_PAIR)]
        sls = [_dot_nt(chains[(p, c)]["lr"], states[p].astype(BF16)) for p in range(N_PAIR)]
        uss = [_dot(chains[(p, c)]["tinv"], (sls[p][:n2] + chains[(p, c)]["av"][:n2]).astype(BF16)).astype(BF16)
               for p in range(N_PAIR)]
        for p in range(N_PAIR):
            ch_ = chains[(p, c)]
            yst = sls[p][n2:] + ch_["av"][n2:] + _dot(ch_["a_rb"], uss[p])
            ys[(p, c)] = yst[:CHUNK] + yst[CHUNK:]
            states[p] = states[p] * ch_["pe"] + _dot_tn(jnp.concatenate([ch_["vs"], uss[p]], axis=0), ch_["kbp"])
            if chunk_is_seq:
                so_ref[c, p] = states[p]
    if not chunk_is_seq:
        for p in range(N_PAIR):
            s_ref[p] = states[p]
            so_ref[0, p] = states[p]

    for q in range(N_PAIR // 2):
        qs = slice(dp * q, dp * (q + 1))
        bonus = bonus_q[q]
        y = jnp.concatenate([jnp.concatenate([ys[(2 * q + hh, c)] for c in range(nc)], axis=0)
                             for hh in range(2)], axis=1)
        mu = _dot_split_r(y, head_ones) * (1.0 / HEAD_DIM)
        d = y - mu
        var = _dot_split_r(d * d, head_ones) * (1.0 / HEAD_DIM)
        yn = d * lax.rsqrt(var + GN_EPS) * par_ref[5:6, qs] + par_ref[6:7, qs]
        y_ref[:, qs] = ((yn + bonus) * _silu(g_ref[:, qs])).astype(y_ref.dtype)


def _rwkv(proj, s0, par, wl, *, row0, n_chunks, chunk_is_seq):
    nc = _tile(n_chunks, 4)
    rows = nc * CHUNK
    blk0 = row0 // rows
    colb = lambda c: pl.BlockSpec((rows, D_B), lambda i, c=c: (blk0 + i, COL_B // D_B + c))
    full = lambda a: pl.BlockSpec(a.shape, lambda i: (0,) * a.ndim)
    sblk = nc if chunk_is_seq else 1
    smap = (lambda i: (i, 0, 0, 0)) if chunk_is_seq else (lambda i: (0, 0, 0, 0))
    state = pl.BlockSpec((sblk, N_PAIR, LANES, LANES), smap)
    return pl.pallas_call(
        functools.partial(_rwkv_kernel, nc=nc, chunk_is_seq=chunk_is_seq),
        grid=(n_chunks // nc,),
        in_specs=[colb(0), colb(1), colb(2), colb(3),
                  pl.BlockSpec((rows, LANES), lambda i: (blk0 + i, COL_LORA // LANES)),
                  state, full(par), full(wl)],
        out_specs=[pl.BlockSpec((rows, D_B), lambda i: (i, 0)), state],
        out_shape=[jax.ShapeDtypeStruct((n_chunks * CHUNK, D_B), BF16),
                   jax.ShapeDtypeStruct(s0.shape, F32)],
        scratch_shapes=[] if chunk_is_seq else [pltpu.VMEM((N_PAIR, LANES, LANES), F32)],
        compiler_params=pltpu.CompilerParams(dimension_semantics=("arbitrary",), vmem_limit_bytes=VMEM_LIMIT),
        name="rwkv7_seqchunks" if chunk_is_seq else "rwkv7_stream",
    )(proj, proj, proj, proj, proj, s0, par, wl)


def _pair_blockdiag(s):
    b = s.shape[0]
    s = s.reshape(b, N_PAIR, 2, HEAD_DIM, HEAD_DIM)
    z = jnp.zeros_like(s[:, :, 0])
    top = jnp.concatenate([s[:, :, 0], z], axis=-1)
    bot = jnp.concatenate([z, s[:, :, 1]], axis=-1)
    return jnp.concatenate([top, bot], axis=-2)


def _pair_unblock(sp):
    b = sp.shape[0]
    h0 = sp[:, :, :HEAD_DIM, :HEAD_DIM]
    h1 = sp[:, :, HEAD_DIM:, HEAD_DIM:]
    return jnp.stack([h0, h1], axis=2).reshape(b, N_HEADS, HEAD_DIM, HEAD_DIM)


def _fox_prep_kernel(x_ref, b_ref, lf_ref, c_ref, carry_ref, *, n_chunks, n_raw):
    rr = lax.broadcasted_iota(jnp.int32, (CUM_W, CUM_W), 0)
    cc = lax.broadcasted_iota(jnp.int32, (CUM_W, CUM_W), 1)
    upper = (rr <= cc).astype(BF16)
    ones = jnp.ones((CUM_W, CUM_W), BF16)
    carry_ref[...] = jnp.zeros_like(carry_ref)

    def step(ci, x):
        lf_ref[0, ci] = x
        c_ref[0, ci] = _dot_exact_r(x, upper) + carry_ref[...]
        carry_ref[...] = carry_ref[...] + _dot_exact_r(x, ones)

    def body_keep(ci, carry):
        step(ci, x_ref[0, ci])
        return carry

    def body_raw(ci, carry):
        step(ci, -_softplus(-(x_ref[0, ci] + b_ref[...])))
        return carry

    lax.fori_loop(0, n_chunks - n_raw, body_keep, 0)
    lax.fori_loop(n_chunks - n_raw, n_chunks, body_raw, 0)


def _fox_prep(x, b, n_raw):
    nb, nch = x.shape[:2]
    spec = pl.BlockSpec((1,) + x.shape[1:], lambda s: (s, 0, 0, 0))
    return pl.pallas_call(
        functools.partial(_fox_prep_kernel, n_chunks=nch, n_raw=n_raw),
        grid=(nb,),
        in_specs=[spec, pl.BlockSpec(b.shape, lambda s: (0, 0))],
        out_specs=[spec, spec],
        out_shape=[jax.ShapeDtypeStruct(x.shape, F32)] * 2,
        scratch_shapes=[pltpu.VMEM(x.shape[2:], F32)],
        compiler_params=pltpu.CompilerParams(dimension_semantics=("arbitrary",)),
        name="fox_logf_cumsum",
    )(x, b)


def _to_time_chunks(x):
    b, t, hh = x.shape
    x = jnp.pad(x, ((0, 0), (0, 0), (0, 16 - hh)))
    return x.reshape(b, t // CUM_W, CUM_W, 16).transpose(0, 1, 3, 2)


def _from_time_chunks(x):
    b, nch = x.shape[:2]
    return x.transpose(0, 1, 3, 2).reshape(b, nch * CUM_W, 16)[:, :, :N_HEADS]


NEG = -1e30
LOG2E = 1.4426950408889634


def _attn_prompt_kernel(q_ref, k_ref, v_ref, c_ref, g_ref, o_ref, m_ref, a_ref, acc_ref, s_ref, *, bq):
    i = pl.program_id(1)
    q = q_ref[...] * (HEAD_DIM ** -0.5 * LOG2E)
    lm0 = lax.broadcasted_iota(jnp.int32, (bq, LANES), 1) < HEAD_DIM
    own = (lm0, jnp.logical_not(lm0))
    row = lax.broadcasted_iota(jnp.int32, (bq, bq), 0)
    col = lax.broadcasted_iota(jnp.int32, (bq, bq), 1)
    causal = col <= row
    qs = (jnp.where(lm0, q, 0.0).astype(BF16), jnp.where(lm0, 0.0, q).astype(BF16))
    one = jnp.ones((bq, LANES), BF16)
    m_ref[...] = jnp.full(m_ref.shape, NEG, F32)
    acc_ref[...] = jnp.zeros(acc_ref.shape, F32)

    def score_pass(jb, h, visible):
        r0 = pl.multiple_of(jb * bq, bq)
        s = _dot_nt(qs[h], k_ref[pl.ds(r0, bq), :]) - c_ref[0, jb, h:h + 1, :] * LOG2E
        if visible is not True:
            s = jnp.where(causal if visible is False else jnp.logical_or(causal, visible), s, NEG)
        s_ref[h] = s
        m_old = m_ref[h]
        m_new = jnp.maximum(m_old, jnp.max(s, axis=1, keepdims=True))
        a_ref[h] = jnp.exp2(m_old - m_new)
        m_ref[h] = m_new

    def value_pass(jb, h):
        r0 = pl.multiple_of(jb * bq, bq)
        vh = jnp.where(own[h], v_ref[pl.ds(r0, bq), :], one)
        pr = jnp.exp2(s_ref[h] - jnp.concatenate([m_ref[h]] * (bq // LANES), axis=1))
        acc_ref[h] = a_ref[h] * acc_ref[h] + _dot(pr.astype(BF16), vh)

    score_pass(0, 0, i > 0)

    def body(jb, carry):
        score_pass(jb, 1, True)
        value_pass(jb, 0)
        score_pass(jb + 1, 0, jb + 1 < i)
        value_pass(jb, 1)
        return carry

    lax.fori_loop(0, i, body, 0)
    score_pass(i, 1, False)
    value_pass(i, 0)
    value_pass(i, 1)
    outs = [acc_ref[h] / pltpu.roll(acc_ref[h], HEAD_DIM, 1) for h in range(2)]
    o = jnp.where(lm0, outs[0], outs[1]) * _silu(g_ref[...])
    o_ref[...] = o.astype(o_ref.dtype)


def _attn_prompt(kv, c, proj, *, t):
    bq = c.shape[-1]
    nq = t // bq
    return pl.pallas_call(
        functools.partial(_attn_prompt_kernel, bq=bq),
        grid=(N_PAIR, nq),
        in_specs=[pl.BlockSpec((bq, LANES), lambda p, i: (i, COL_C // LANES + p)),
                  pl.BlockSpec((t, LANES), lambda p, i: (0, p)),
                  pl.BlockSpec((t, LANES), lambda p, i: (0, N_PAIR + p)),
                  pl.BlockSpec((1, nq, 2, bq), lambda p, i: (p, 0, 0, 0)),
                  pl.BlockSpec((bq, LANES), lambda p, i: (i, (COL_C + 3 * D_C) // LANES + p))],
        out_specs=pl.BlockSpec((bq, LANES), lambda p, i: (i, p)),
        out_shape=jax.ShapeDtypeStruct((t, D_C), BF16),
        scratch_shapes=[pltpu.VMEM((2, bq, LANES), F32), pltpu.VMEM((2, bq, LANES), F32),
                        pltpu.VMEM((2, bq, LANES), F32), pltpu.VMEM((2, bq, bq), F32)],
        compiler_params=pltpu.CompilerParams(
            dimension_semantics=("arbitrary", "arbitrary"), vmem_limit_bytes=VMEM_LIMIT),
        name="fox_attn_prompt",
    )(proj, kv, kv, c, proj)


def _attn_sample_kernel(q_ref, kn_ref, vn_ref, g_ref, kc_ref, vc_ref, c_ref, o_ref, *, past, ts):
    q = q_ref[...].astype(BF16)
    lm0 = lax.broadcasted_iota(jnp.int32, (ts, LANES), 1) < HEAD_DIM
    row = lax.broadcasted_iota(jnp.int32, (ts, ts), 0)
    col = lax.broadcasted_iota(jnp.int32, (ts, ts), 1)
    scale = jnp.asarray(HEAD_DIM ** -0.5, BF16)
    kc = kc_ref[0].astype(BF16)
    vc = vc_ref[0].astype(BF16)
    kn = kn_ref[...].astype(BF16)
    vn = vn_ref[...].astype(BF16)
    outs = []
    for h in range(2):
        qh = jnp.where(lm0 if h == 0 else jnp.logical_not(lm0), q, jnp.zeros_like(q)) * scale
        s1 = _dot_nt(qh, kc) - c_ref[0, 0, h:h + 1, 0:past]
        s2 = _dot_nt(qh, kn) - c_ref[0, 0, h:h + 1, past:past + ts]
        s2 = jnp.where(col <= row, s2, NEG)
        m = jnp.maximum(jnp.max(s1, axis=1, keepdims=True), jnp.max(s2, axis=1, keepdims=True))
        p1 = jnp.exp(s1 - m)
        p2 = jnp.exp(s2 - m)
        l = jnp.sum(p1, axis=1, keepdims=True) + jnp.sum(p2, axis=1, keepdims=True)
        outs.append((_dot(p1.astype(BF16), vc) + _dot(p2.astype(BF16), vn)) / l)
    o = jnp.where(lm0, outs[0], outs[1]) * _silu(g_ref[...])
    o_ref[...] = o.astype(o_ref.dtype)


def _attn_sample(proj, cache_k, cache_v, c, *, row0, layer, nseq, ts):
    past = cache_k.shape[1]
    blk0 = row0 // ts
    pcol = lambda base: pl.BlockSpec((ts, LANES), lambda s, p, base=base: (blk0 + s, base // LANES + p))
    cache = pl.BlockSpec((1, past, LANES), lambda s, p: (layer * nseq + s, 0, p))
    return pl.pallas_call(
        functools.partial(_attn_sample_kernel, past=past, ts=ts),
        grid=(nseq, N_PAIR),
        in_specs=[pcol(COL_C), pcol(COL_C + D_C), pcol(COL_C + 2 * D_C), pcol(COL_C + 3 * D_C),
                  cache, cache,
                  pl.BlockSpec((1, 1, 2, c.shape[-1]), lambda s, p: (s, p, 0, 0))],
        out_specs=pl.BlockSpec((ts, LANES), lambda s, p: (s, p)),
        out_shape=jax.ShapeDtypeStruct((nseq * ts, D_C), BF16),
        compiler_params=pltpu.CompilerParams(
            dimension_semantics=("arbitrary", "arbitrary"), vmem_limit_bytes=VMEM_LIMIT),
        name="fox_attn_sample",
    )(proj, proj, proj, proj, cache_k, cache_v, c)


def _outproj_kernel(ya_ref, yb_ref, yc_ref, h_ref, w_ref, g_ref, o_ref):
    mix = (_dot(ya_ref[...], w_ref[0:D_A, :])
           + _dot(yb_ref[...], w_ref[D_A:D_A + D_B, :])
           + _dot(yc_ref[...], w_ref[D_A + D_B:, :]))
    ms = jnp.mean(mix * mix, axis=-1, keepdims=True)
    o_ref[...] = h_ref[...] + (mix * lax.rsqrt(ms + RMS_EPS)) * g_ref[...]


def _outproj(ya, yb, yc, h, w_out, g):
    m, d = h.shape
    tm = _tile(m, 512)
    rows = lambda w: pl.BlockSpec((tm, w), lambda i: (i, 0))
    return pl.pallas_call(
        _outproj_kernel,
        grid=(m // tm,),
        in_specs=[rows(D_A), rows(D_B), rows(D_C), rows(d),
                  pl.BlockSpec(w_out.shape, lambda i: (0, 0)),
                  pl.BlockSpec((1, d), lambda i: (0, 0))],
        out_specs=rows(d),
        out_shape=jax.ShapeDtypeStruct((m, d), F32),
        compiler_params=pltpu.CompilerParams(
            dimension_semantics=("arbitrary",), vmem_limit_bytes=VMEM_LIMIT),
        name="outproj_postnorm",
    )(ya, yb, yc, h, w_out, g)


def _ple_kernel(h_ref, p_ref, wg_ref, wp_ref, o_ref):
    h = h_ref[...]
    gate = jax.nn.sigmoid(_dot(h.astype(BF16), wg_ref[...]))
    o_ref[...] = h + gate * _dot(p_ref[...].astype(BF16), wp_ref[...])


def _ple(h, p, w_gate, w_proj):
    m, d = h.shape
    tm = _tile(m, 512)
    return pl.pallas_call(
        _ple_kernel,
        grid=(m // tm,),
        in_specs=[pl.BlockSpec((tm, d), lambda i: (i, 0)),
                  pl.BlockSpec((tm, D_PLE), lambda i: (i, 0)),
                  pl.BlockSpec(w_gate.shape, lambda i: (0, 0)),
                  pl.BlockSpec(w_proj.shape, lambda i: (0, 0))],
        out_specs=pl.BlockSpec((tm, d), lambda i: (i, 0)),
        out_shape=jax.ShapeDtypeStruct((m, d), F32),
        compiler_params=pltpu.CompilerParams(
            dimension_semantics=("arbitrary",), vmem_limit_bytes=VMEM_LIMIT),
        name="ple_gate",
    )(h, p, w_gate, w_proj)


def kernel(x_prompt, x_sample, p_prompt, p_sample, cache_k, cache_v, cache_logf, state_conv, state_wkv, norm_pre_g, norm_post_g, w_in, b_f, conv_w, conv_b, conv_ln_g, conv_ln_b, rwkv_w0, rwkv_w1, rwkv_w2, rwkv_a0, rwkv_a1, rwkv_a2, rwkv_kk, rwkv_ka, rwkv_rk, rwkv_ln_g, rwkv_ln_b, w_out, ple_proj, ple_gate):
    depth = w_in.shape[0]
    bp, tp, d = x_prompt.shape
    bs, ts, _ = x_sample.shape
    past = cache_k.shape[2]
    n_p = bp * tp
    n_s = bs * ts
    assert tp % CHUNK == 0 and ts == CHUNK and past % CUM_W == 0 and tp % CUM_W == 0 and bp == 1
    bq = _tile(tp, ATTN_BLOCK)

    zpad = lambda n: jnp.zeros((depth, d, n), F32)
    w_ext = jnp.concatenate(
        [w_in, zpad(COL_LORA - w_in.shape[2]), rwkv_w1, rwkv_a1, zpad(N_EXT - COL_LORA - 2 * R_LORA)],
        axis=2).astype(BF16)
    zl = jnp.zeros((depth, R_LORA, D_B), F32)
    w_lora = jnp.concatenate([jnp.concatenate([rwkv_w2, zl], axis=2),
                              jnp.concatenate([zl, rwkv_a2], axis=2)], axis=1).astype(BF16)
    par = jnp.stack([rwkv_w0, rwkv_a0, rwkv_kk, rwkv_ka, rwkv_rk.reshape(depth, D_B), rwkv_ln_g, rwkv_ln_b,
                     jnp.zeros_like(rwkv_w0)], axis=1)
    w_out_b = w_out.astype(BF16)
    ple_gate_b = ple_gate.astype(BF16)
    ple_proj_b = ple_proj.astype(BF16)
    b_f16 = jnp.broadcast_to(jnp.pad(b_f, ((0, 0), (0, 16 - N_HEADS)))[:, :, None], (depth, 16, CUM_W))
    cache_k2 = cache_k.reshape(depth * bs, past, D_C)
    cache_v2 = cache_v.reshape(depth * bs, past, D_C)

    h = jnp.concatenate([x_prompt.reshape(n_p, d), x_sample.reshape(n_s, d)], axis=0)
    p_all = jnp.concatenate([p_prompt.reshape(depth, n_p, D_PLE), p_sample.reshape(depth, n_s, D_PLE)], axis=1)
    zero_conv = jnp.zeros((bp, K_CONV - 1, D_A), F32)
    zero_wkv = jnp.zeros((bp, N_PAIR, LANES, LANES), F32)

    outs = {name: [] for name in ("conv_p", "wkv_p", "k_p", "v_p", "lf_p", "conv_s", "wkv_s", "k_s", "v_s", "lf_s")}
    for l in range(depth):
        proj = _inproj(h, norm_pre_g[l][None], w_ext[l])

        ya_p, conv_p = _conv(proj, zero_conv, conv_w[l], conv_b[l][None], conv_ln_g[l][None],
                             conv_ln_b[l][None], row0=0, nseq=bp, t_seq=tp)
        ya_s, conv_s = _conv(proj, state_conv[l], conv_w[l], conv_b[l][None], conv_ln_g[l][None],
                             conv_ln_b[l][None], row0=n_p, nseq=bs, t_seq=ts)
        ya = jnp.concatenate([ya_p, ya_s], axis=0)

        yb_p, wkv_p = _rwkv(proj, zero_wkv, par[l], w_lora[l], row0=0, n_chunks=tp // CHUNK, chunk_is_seq=False)
        yb_s, wkv_s = _rwkv(proj, _pair_blockdiag(state_wkv[l]), par[l], w_lora[l],
                            row0=n_p, n_chunks=bs, chunk_is_seq=True)
        yb = jnp.concatenate([yb_p, yb_s], axis=0)

        k_new = proj[:, COL_C + D_C:COL_C + 2 * D_C]
        v_new = proj[:, COL_C + 2 * D_C:COL_C + 3 * D_C]
        f_raw = proj[:, COL_F:COL_F + N_HEADS]

        lf_pc, c_pc = _fox_prep(_to_time_chunks(f_raw[:n_p].reshape(bp, tp, N_HEADS)), b_f16[l], tp // CUM_W)
        lf_p = _from_time_chunks(lf_pc)
        c_p = c_pc[0].transpose(1, 0, 2).reshape(16, tp)[:N_HEADS]
        c_p = c_p.reshape(N_PAIR, 2, tp // bq, bq).transpose(0, 2, 1, 3)
        kv_p = proj[:n_p, COL_C + D_C:COL_C + 3 * D_C].astype(BF16)
        yc_p = _attn_prompt(kv_p, c_p, proj, t=tp)

        x_s = jnp.concatenate(
            [_to_time_chunks(cache_logf[l]),
             _to_time_chunks(jnp.pad(f_raw[n_p:].reshape(bs, ts, N_HEADS), ((0, 0), (0, CUM_W - ts), (0, 0))))],
            axis=1)
        lf_sc, c_sc = _fox_prep(x_s, b_f16[l], 1)
        lf_s = _from_time_chunks(lf_sc[:, -1:])[:, :ts]
        c_s = c_sc.transpose(0, 2, 1, 3).reshape(bs, 16, past + CUM_W)[:, :N_HEADS]
        c_s = c_s.reshape(bs, N_PAIR, 2, past + CUM_W)
        yc_s = _attn_sample(proj, cache_k2, cache_v2, c_s, row0=n_p, layer=l, nseq=bs, ts=ts)
        yc = jnp.concatenate([yc_p, yc_s], axis=0)

        h = _outproj(ya, yb, yc, h, w_out_b[l], norm_post_g[l][None])
        h = _ple(h, p_all[l], ple_gate_b[l], ple_proj_b[l])

        outs["conv_p"].append(conv_p)
        outs["conv_s"].append(conv_s)
        outs["wkv_p"].append(_pair_unblock(wkv_p))
        outs["wkv_s"].append(_pair_unblock(wkv_s))
        outs["k_p"].append(k_new[:n_p].reshape(bp, tp, N_HEADS, HEAD_DIM))
        outs["v_p"].append(v_new[:n_p].reshape(bp, tp, N_HEADS, HEAD_DIM))
        outs["k_s"].append(k_new[n_p:].reshape(bs, ts, N_HEADS, HEAD_DIM))
        outs["v_s"].append(v_new[n_p:].reshape(bs, ts, N_HEADS, HEAD_DIM))
        outs["lf_p"].append(lf_p)
        outs["lf_s"].append(lf_s)

    st = lambda name: jnp.stack(outs[name])
    return (h[:n_p].reshape(bp, tp, d), h[n_p:].reshape(bs, ts, d),
            st("conv_p"), st("wkv_p"), st("k_p"), st("v_p"), st("lf_p"),
            st("conv_s"), st("wkv_s"), st("k_s"), st("v_s"), st("lf_s"))
```

```python
import functools

import jax
import jax.numpy as jnp
from jax import lax
from jax.experimental import pallas as pl
from jax.experimental.pallas import tpu as pltpu

F32 = jnp.float32
BF16 = jnp.bfloat16

D_MODEL = 2048
D_PLE = 256
HEAD_DIM = 64
D_A = 512
N_HEADS = 12
D_B = N_HEADS * HEAD_DIM
D_C = N_HEADS * HEAD_DIM
K_CONV = 31
R_LORA = 64
RMS_EPS = 1e-6
LN_EPS = 1e-5
GN_EPS = 64e-5

LANES = 128
N_PAIR = N_HEADS // 2
CHUNK = 64
CONV_HALO = 32
CUM_W = 128
ATTN_BLOCK = 512

COL_A = 0
COL_B = 3 * D_A
COL_C = COL_B + 4 * D_B
COL_F = COL_C + 4 * D_C
COL_LORA = COL_F + LANES
N_EXT = 8192
VMEM_LIMIT = 52 * 1024 * 1024


def _tile(n, pref):
    t = min(pref, n)
    while n % t:
        t //= 2
    return t


def _dot(a, b):
    return jnp.dot(a, b, preferred_element_type=F32)


def _dot_nt(a, b):
    return lax.dot_general(a, b, (((1,), (1,)), ((), ())), preferred_element_type=F32)


def _dot_tn(a, b):
    return lax.dot_general(a, b, (((0,), (0,)), ((), ())), preferred_element_type=F32)


def _split3(x):
    hi = x.astype(BF16)
    r1 = x - hi.astype(F32)
    mid = r1.astype(BF16)
    lo = (r1 - mid.astype(F32)).astype(BF16)
    return hi, mid, lo


def _dot_exact_r(x, m):
    hi, mid, lo = _split3(x)
    return _dot(hi, m) + _dot(mid, m) + _dot(lo, m)


def _dot_split_r(x, m):
    hi = x.astype(BF16)
    lo = (x - hi.astype(F32)).astype(BF16)
    return _dot(hi, m) + _dot(lo, m)


def _dot_split_l(m, x):
    hi = x.astype(BF16)
    lo = (x - hi.astype(F32)).astype(BF16)
    return _dot(m, hi) + _dot(m, lo)


def _softplus(z):
    return jnp.maximum(z, 0.0) + jnp.log(1.0 + jnp.exp(-jnp.abs(z)))


def _silu(z):
    return z * jax.nn.sigmoid(z)


def _inproj_kernel(x_ref, g_ref, w_ref, o_ref):
    x = x_ref[...]
    ms = jnp.mean(x * x, axis=-1, keepdims=True)
    xn = (x * lax.rsqrt(ms + RMS_EPS)) * g_ref[...]
    o_ref[...] = _dot(xn.astype(BF16), w_ref[...])


def _inproj(h, g, w_ext, layer):
    m, d = h.shape
    n = w_ext.shape[2]
    tm = _tile(m, 512)
    tn = 2048
    return pl.pallas_call(
        _inproj_kernel,
        grid=(n // tn, m // tm),
        in_specs=[pl.BlockSpec((tm, d), lambda j, i: (i, 0)),
                  pl.BlockSpec((1, d), lambda j, i: (0, 0)),
                  pl.BlockSpec((None, d, tn), lambda j, i: (layer, 0, j))],
        out_specs=pl.BlockSpec((tm, tn), lambda j, i: (i, j)),
        out_shape=jax.ShapeDtypeStruct((m, n), F32),
        compiler_params=pltpu.CompilerParams(
            dimension_semantics=("arbitrary", "arbitrary"), vmem_limit_bytes=VMEM_LIMIT),
        name="inproj",
    )(h, g, w_ext)


def _conv_kernel(av_ref, ag_ref, gate_ref, st_ref, w_ref, b_ref, lg_ref, lb_ref,
                 y_ref, sto_ref, ext_ref, *, tt, rb):
    t = pl.program_id(1)

    @pl.when(t == 0)
    def _init():
        ext_ref[0:8, :] = jnp.zeros((8, D_A), F32)
        ext_ref[CONV_HALO - (K_CONV - 1):CONV_HALO, :] = st_ref[0]

    ext_ref[CONV_HALO:CONV_HALO + tt, :] = av_ref[...] * jax.nn.sigmoid(ag_ref[...])
    off = CONV_HALO - (K_CONV - 1)

    for i in range(tt // rb):
        r0 = i * rb
        ys = []
        for c in range(D_A // LANES):
            cs = slice(LANES * c, LANES * (c + 1))
            acc = jnp.broadcast_to(b_ref[:, cs], (rb, LANES))
            for j in range(K_CONV):
                acc = acc + w_ref[j:j + 1, cs] * ext_ref[r0 + off + j:r0 + off + j + rb, cs]
            ys.append(acc)
        y = jnp.concatenate(ys, axis=1)
        mu = jnp.mean(y, axis=-1, keepdims=True)
        d = y - mu
        var = jnp.mean(d * d, axis=-1, keepdims=True)
        yn = d * lax.rsqrt(var + LN_EPS) * lg_ref[...] + lb_ref[...]
        out = _silu(yn) * _silu(gate_ref[r0:r0 + rb, :])
        y_ref[r0:r0 + rb, :] = out.astype(y_ref.dtype)
    sto_ref[0] = ext_ref[tt + off:tt + CONV_HALO, :]
    ext_ref[0:CONV_HALO, :] = ext_ref[tt:tt + CONV_HALO, :]


def _conv(proj, state, w, b, lg, lb, *, row0, nseq, t_seq):
    tt = _tile(t_seq, 256)
    rb = min(64, tt)
    nt = t_seq // tt
    blk0 = row0 // tt
    col = lambda c: pl.BlockSpec((tt, D_A), lambda s, t, c=c: (blk0 + s * nt + t, c))
    vec = lambda r: pl.BlockSpec((r, D_A), lambda s, t: (0, 0))
    y, st = pl.pallas_call(
        functools.partial(_conv_kernel, tt=tt, rb=rb),
        grid=(nseq, nt),
        in_specs=[col(0), col(1), col(2),
                  pl.BlockSpec((1, K_CONV - 1, D_A), lambda s, t: (s, 0, 0)),
                  vec(K_CONV), vec(1), vec(1), vec(1)],
        out_specs=[pl.BlockSpec((tt, D_A), lambda s, t: (s * nt + t, 0)),
                   pl.BlockSpec((1, K_CONV - 1, D_A), lambda s, t: (s, 0, 0))],
        out_shape=[jax.ShapeDtypeStruct((nseq * t_seq, D_A), BF16),
                   jax.ShapeDtypeStruct((nseq, K_CONV - 1, D_A), F32)],
        scratch_shapes=[pltpu.VMEM((tt + CONV_HALO, D_A), F32)],
        compiler_params=pltpu.CompilerParams(dimension_semantics=("arbitrary", "arbitrary")),
        name="conv_branch",
    )(proj, proj, proj, state, w, b, lg, lb)
    return y, st


def _rwkv_kernel(r_ref, k_ref, v_ref, g_ref, wa_ref, s0_ref, par_ref, wl_ref,
                 y_ref, so_ref, *scratch, nc, chunk_is_seq):
    rows = nc * CHUNK
    n2 = 2 * CHUNK
    dp = 2 * LANES
    ri = lax.broadcasted_iota(jnp.int32, (n2, n2), 0)
    ci = lax.broadcasted_iota(jnp.int32, (n2, n2), 1)
    same = (ri >= CHUNK) == (ci >= CHUNK)
    m_strict = same & (ci < ri)
    m_incl = same & (ci <= ri)
    eye = (ri == ci).astype(F32)
    rt = lax.broadcasted_iota(jnp.int32, (rows, rows), 0)
    ct = lax.broadcasted_iota(jnp.int32, (rows, rows), 1)
    tri = (((rt // CHUNK) == (ct // CHUNK)) & (ct <= rt)).astype(BF16)
    rh = lax.broadcasted_iota(jnp.int32, (dp, dp), 0)
    ch = lax.broadcasted_iota(jnp.int32, (dp, dp), 1)
    head_ones = ((rh // HEAD_DIM) == (ch // HEAD_DIM)).astype(BF16)
    lm0 = lax.broadcasted_iota(jnp.int32, (CHUNK, LANES), 1) < HEAD_DIM

    def stack(z):
        return jnp.concatenate([jnp.where(lm0, z, 0.0), jnp.where(lm0, 0.0, z)], axis=0).astype(BF16)

    def dup(z):
        zb = z.astype(BF16)
        return jnp.concatenate([zb, zb], axis=0)

    if not chunk_is_seq:
        s_ref = scratch[0]

        @pl.when(pl.program_id(0) == 0)
        def _init():
            s_ref[...] = s0_ref[0]

    wa = wa_ref[...]
    lane = lax.broadcasted_iota(jnp.int32, wa.shape, 1)
    lora = _dot(jnp.where(lane < R_LORA, jnp.tanh(wa), wa).astype(BF16), wl_ref[...])
    lw_all = -jnp.exp(-_softplus(-(par_ref[0:1, :] + lora[:, :D_B])) - 0.5)
    a_all = jax.nn.sigmoid(par_ref[1:2, :] + lora[:, D_B:])
    cl_all = _dot_split_l(tri, lw_all)

    chains = {}
    bonus_q = []
    for q in range(N_PAIR // 2):
        qs = slice(dp * q, dp * (q + 1))
        r = r_ref[:, qs]
        k = k_ref[:, qs]
        v = v_ref[:, qs]
        a = a_all[:, qs]
        lw = lw_all[:, qs]
        cl = cl_all[:, qs]
        kk = k * par_ref[2:3, qs]
        kh = k * (1.0 + (a - 1.0) * par_ref[3:4, qs])
        sums = _dot_split_r(jnp.concatenate([kk * kk, r * kh * par_ref[4:5, qs]], axis=0), head_ones)
        kk = kk / jnp.maximum(jnp.sqrt(sums[:rows]), 1e-12)
        bonus = sums[rows:] * v
        b = kk * a
        at = -(kk * jnp.exp(cl - lw))
        rt_ = r * jnp.exp(cl)
        p_inv = jnp.exp(-cl)
        kt = kh * p_inv
        bt = b * p_inv
        cl_last = jnp.concatenate(
            [jnp.broadcast_to(cl[CHUNK * (c + 1) - 1:CHUNK * (c + 1), :], (CHUNK, dp)) for c in range(nc)], axis=0)
        p_rem = jnp.exp(cl_last - cl)
        kp = kh * p_rem
        bp = b * p_rem
        p_end = jnp.exp(cl_last)

        bonus_q.append(bonus)
        for hh in range(2):
            ls_ = slice(LANES * hh, LANES * (hh + 1))
            for c in range(nc):
                rs_ = slice(CHUNK * c, CHUNK * (c + 1))
                ch_ = dict(
                    lr=jnp.concatenate([stack(at[rs_, ls_]), stack(rt_[rs_, ls_])], axis=0),
                    bk=jnp.concatenate([dup(bt[rs_, ls_]), dup(kt[rs_, ls_])], axis=0),
                    vs=stack(v[rs_, ls_]),
                    kbp=jnp.concatenate([stack(kp[rs_, ls_]), stack(bp[rs_, ls_])], axis=0),
                    pe=p_end[CHUNK * c:CHUNK * c + 1, ls_])
                chains[(2 * q + hh, c)] = ch_

    order = [(p, c) for c in range(nc) for p in range(N_PAIR)]
    for key in order:
        ch_ = chains[key]
        gm = _dot_nt(ch_["lr"], ch_.pop("bk"))
        x = jnp.where(m_strict, gm[:n2, :n2], 0.0)
        ch_["a_k"] = jnp.concatenate([jnp.where(m_strict, gm[:n2, n2:], 0.0),
                                      jnp.where(m_incl, gm[n2:, n2:], 0.0)], axis=0).astype(BF16)
        ch_["a_rb"] = jnp.where(m_incl, gm[n2:, :n2], 0.0).astype(BF16)
        ch_["tinv"] = eye + x
        ch_["xb"] = x.astype(BF16)
    for key in order:
        ch_ = chains[key]
        xb = ch_.pop("xb")
        ch_["xx"] = _dot(xb, xb).astype(BF16)
        ch_["av"] = _dot(ch_.pop("a_k"), ch_["vs"])
    for _ in range(4):
        for key in order:
            ch_ = chains[key]
            both = _dot(ch_["xx"], jnp.concatenate([ch_["tinv"].astype(BF16), ch_["xx"]], axis=1))
            ch_["tinv"] = ch_["tinv"] + both[:, :n2]
            ch_["xx"] = both[:, n2:].astype(BF16)
    for key in order:
        ch_ = chains[key]
        ch_["tinv"] = (ch_["tinv"] + _dot(ch_.pop("xx"), ch_["tinv"].astype(BF16))).astype(BF16)

    states = [None if chunk_is_seq else s_ref[p] for p in range(N_PAIR)]
    ys = {}
    for c in range(nc):
        if chunk_is_seq:
            states = [s0_ref[c, p] for p in range(N_PAIR)]
        sls = [_dot_nt(chains[(p, c)]["lr"], states[p].astype(BF16)) for p in range(N_PAIR)]
        uss = [_dot(chains[(p, c)]["tinv"], (sls[p][:n2] + chains[(p, c)]["av"][:n2]).astype(BF16)).astype(BF16)
               for p in range(N_PAIR)]
        for p in range(N_PAIR):
            ch_ = chains[(p, c)]
            yst = sls[p][n2:] + ch_["av"][n2:] + _dot(ch_["a_rb"], uss[p])
            ys[(p, c)] = yst[:CHUNK] + yst[CHUNK:]
            states[p] = states[p] * ch_["pe"] + _dot_tn(jnp.concatenate([ch_["vs"], uss[p]], axis=0), ch_["kbp"])
            if chunk_is_seq:
                so_ref[c, p] = states[p]
    if not chunk_is_seq:
        for p in range(N_PAIR):
            s_ref[p] = states[p]
            so_ref[0, p] = states[p]

    for q in range(N_PAIR // 2):
        qs = slice(dp * q, dp * (q + 1))
        bonus = bonus_q[q]
        y = jnp.concatenate([jnp.concatenate([ys[(2 * q + hh, c)] for c in range(nc)], axis=0)
                             for hh in range(2)], axis=1)
        mu = _dot_split_r(y, head_ones) * (1.0 / HEAD_DIM)
        d = y - mu
        var = _dot_split_r(d * d, head_ones) * (1.0 / HEAD_DIM)
        yn = d * lax.rsqrt(var + GN_EPS) * par_ref[5:6, qs] + par_ref[6:7, qs]
        y_ref[:, qs] = ((yn + bonus) * _silu(g_ref[:, qs])).astype(y_ref.dtype)


def _rwkv(proj, s0, par, wl, *, row0, n_chunks, chunk_is_seq):
    nc = _tile(n_chunks, 4)
    rows = nc * CHUNK
    blk0 = row0 // rows
    colb = lambda c: pl.BlockSpec((rows, D_B), lambda i, c=c: (blk0 + i, COL_B // D_B + c))
    full = lambda a: pl.BlockSpec(a.shape, lambda i: (0,) * a.ndim)
    sblk = nc if chunk_is_seq else 1
    smap = (lambda i: (i, 0, 0, 0)) if chunk_is_seq else (lambda i: (0, 0, 0, 0))
    state = pl.BlockSpec((sblk, N_PAIR, LANES, LANES), smap)
    return pl.pallas_call(
        functools.partial(_rwkv_kernel, nc=nc, chunk_is_seq=chunk_is_seq),
        grid=(n_chunks // nc,),
        in_specs=[colb(0), colb(1), colb(2), colb(3),
                  pl.BlockSpec((rows, LANES), lambda i: (blk0 + i, COL_LORA // LANES)),
                  state, full(par), full(wl)],
        out_specs=[pl.BlockSpec((rows, D_B), lambda i: (i, 0)), state],
        out_shape=[jax.ShapeDtypeStruct((n_chunks * CHUNK, D_B), BF16),
                   jax.ShapeDtypeStruct(s0.shape, F32)],
        scratch_shapes=[] if chunk_is_seq else [pltpu.VMEM((N_PAIR, LANES, LANES), F32)],
        compiler_params=pltpu.CompilerParams(dimension_semantics=("arbitrary",), vmem_limit_bytes=VMEM_LIMIT),
        name="rwkv7_seqchunks" if chunk_is_seq else "rwkv7_stream",
    )(proj, proj, proj, proj, proj, s0, par, wl)


def _pair_blockdiag(s):
    b = s.shape[0]
    s = s.reshape(b, N_PAIR, 2, HEAD_DIM, HEAD_DIM)
    z = jnp.zeros_like(s[:, :, 0])
    top = jnp.concatenate([s[:, :, 0], z], axis=-1)
    bot = jnp.concatenate([z, s[:, :, 1]], axis=-1)
    return jnp.concatenate([top, bot], axis=-2)


def _pair_unblock(sp):
    b = sp.shape[0]
    h0 = sp[:, :, :HEAD_DIM, :HEAD_DIM]
    h1 = sp[:, :, HEAD_DIM:, HEAD_DIM:]
    return jnp.stack([h0, h1], axis=2).reshape(b, N_HEADS, HEAD_DIM, HEAD_DIM)


def _fox_prep_kernel(x_ref, b_ref, lf_ref, c_ref, carry_ref, *, n_chunks, n_raw):
    rr = lax.broadcasted_iota(jnp.int32, (CUM_W, CUM_W), 0)
    cc = lax.broadcasted_iota(jnp.int32, (CUM_W, CUM_W), 1)
    upper = (rr <= cc).astype(BF16)
    ones = jnp.ones((CUM_W, CUM_W), BF16)
    carry_ref[...] = jnp.zeros_like(carry_ref)

    def step(ci, x):
        lf_ref[0, ci] = x
        c_ref[0, ci] = _dot_exact_r(x, upper) + carry_ref[...]
        carry_ref[...] = carry_ref[...] + _dot_exact_r(x, ones)

    def body_keep(ci, carry):
        step(ci, x_ref[0, ci])
        return carry

    def body_raw(ci, carry):
        step(ci, -_softplus(-(x_ref[0, ci] + b_ref[...])))
        return carry

    lax.fori_loop(0, n_chunks - n_raw, body_keep, 0)
    lax.fori_loop(n_chunks - n_raw, n_chunks, body_raw, 0)


def _fox_prep(x, b, n_raw):
    nb, nch = x.shape[:2]
    spec = pl.BlockSpec((1,) + x.shape[1:], lambda s: (s, 0, 0, 0))
    return pl.pallas_call(
        functools.partial(_fox_prep_kernel, n_chunks=nch, n_raw=n_raw),
        grid=(nb,),
        in_specs=[spec, pl.BlockSpec(b.shape, lambda s: (0, 0))],
        out_specs=[spec, spec],
        out_shape=[jax.ShapeDtypeStruct(x.shape, F32)] * 2,
        scratch_shapes=[pltpu.VMEM(x.shape[2:], F32)],
        compiler_params=pltpu.CompilerParams(dimension_semantics=("arbitrary",)),
        name="fox_logf_cumsum",
    )(x, b)


def _to_time_chunks(x):
    b, t, hh = x.shape
    x = jnp.pad(x, ((0, 0), (0, 0), (0, 16 - hh)))
    return x.reshape(b, t // CUM_W, CUM_W, 16).transpose(0, 1, 3, 2)


def _from_time_chunks(x):
    b, nch = x.shape[:2]
    return x.transpose(0, 1, 3, 2).reshape(b, nch * CUM_W, 16)[:, :, :N_HEADS]


NEG = -1e30
LOG2E = 1.4426950408889634
PRUNE_LOG2 = 160.0


def _attn_stats_kernel(q_ref, k_ref, o_ref):
    q = q_ref[...]
    k = k_ref[...]
    n = q.shape[0]
    rr = lax.broadcasted_iota(jnp.int32, (LANES, LANES), 0)
    cc = lax.broadcasted_iota(jnp.int32, (LANES, LANES), 1)
    head_ones = ((rr // HEAD_DIM) == (cc // HEAD_DIM)).astype(BF16)
    sums = _dot_split_r(jnp.concatenate([q * q, k * k, q * k], axis=0), head_ones)
    qmax = jnp.sqrt(jnp.max(sums[:n], axis=0, keepdims=True))
    kmax = jnp.sqrt(jnp.max(sums[n:2 * n], axis=0, keepdims=True))
    dneg = jnp.max(-sums[2 * n:], axis=0, keepdims=True)
    o_ref[0, 0] = jnp.concatenate([qmax, kmax, dneg, jnp.zeros((5, LANES), F32)], axis=0)


def _attn_stats(proj, *, t, bq):
    nq = t // bq
    return pl.pallas_call(
        _attn_stats_kernel,
        grid=(N_PAIR, nq),
        in_specs=[pl.BlockSpec((bq, LANES), lambda p, i: (i, COL_C // LANES + p)),
                  pl.BlockSpec((bq, LANES), lambda p, i: (i, (COL_C + D_C) // LANES + p))],
        out_specs=pl.BlockSpec((1, 1, 8, LANES), lambda p, i: (p, i, 0, 0)),
        out_shape=jax.ShapeDtypeStruct((N_PAIR, nq, 8, LANES), F32),
        compiler_params=pltpu.CompilerParams(dimension_semantics=("arbitrary", "arbitrary")),
        name="fox_attn_stats",
    )(proj, proj)


def _first_key_block(stats, c):
    nq = c.shape[1]
    sc = HEAD_DIM ** -0.5 * LOG2E
    qmax = stats[:, :, 0, ::HEAD_DIM]
    kpre = lax.cummax(stats[:, :, 1, ::HEAD_DIM], axis=1)
    dneg = stats[:, :, 2, ::HEAD_DIM] * sc
    c2s = c[:, :, :, 0] * LOG2E
    c2e = c[:, :, :, -1] * LOG2E
    bound = (1.04 * sc * qmax[:, :, None, :] * kpre[:, None, :, :] + dneg[:, :, None, :]
             + c2s[:, :, None, :] - c2e[:, None, :, :])
    before = jnp.arange(nq)[None, :, None] > jnp.arange(nq)[None, None, :]
    skip = jnp.all(bound < -PRUNE_LOG2, axis=-1) & before
    return jnp.sum(jnp.cumprod(skip.astype(jnp.int32), axis=2), axis=2).astype(jnp.int32).reshape(-1)


def _attn_prompt_kernel(jb0_ref, q_ref, k_ref, v_ref, c_ref, g_ref, o_ref, m_ref, a_ref, acc_ref, s_ref, *, bq):
    i = pl.program_id(1)
    q = q_ref[...] * (HEAD_DIM ** -0.5 * LOG2E)
    lm0 = lax.broadcasted_iota(jnp.int32, (bq, LANES), 1) < HEAD_DIM
    own = (lm0, jnp.logical_not(lm0))
    row = lax.broadcasted_iota(jnp.int32, (bq, bq), 0)
    col = lax.broadcasted_iota(jnp.int32, (bq, bq), 1)
    causal = col <= row
    qs = (jnp.where(lm0, q, 0.0).astype(BF16), jnp.where(lm0, 0.0, q).astype(BF16))
    one = jnp.ones((bq, LANES), BF16)
    m_ref[...] = jnp.full(m_ref.shape, NEG, F32)
    acc_ref[...] = jnp.zeros(acc_ref.shape, F32)

    def score_pass(jb, h, visible):
        r0 = pl.multiple_of(jb * bq, bq)
        s = _dot_nt(qs[h], k_ref[pl.ds(r0, bq), :]) - c_ref[0, jb, h:h + 1, :] * LOG2E
        if visible is not True:
            s = jnp.where(causal if visible is False else jnp.logical_or(causal, visible), s, NEG)
        s_ref[h] = s
        m_old = m_ref[h]
        m_new = jnp.maximum(m_old, jnp.max(s, axis=1, keepdims=True))
        a_ref[h] = jnp.exp2(m_old - m_new)
        m_ref[h] = m_new

    def value_pass(jb, h):
        r0 = pl.multiple_of(jb * bq, bq)
        vh = jnp.where(own[h], v_ref[pl.ds(r0, bq), :], one)
        pr = jnp.exp2(s_ref[h] - jnp.concatenate([m_ref[h]] * (bq // LANES), axis=1))
        acc_ref[h] = a_ref[h] * acc_ref[h] + _dot(pr.astype(BF16), vh)

    jb0 = jb0_ref[pl.program_id(0) * pl.num_programs(1) + i]
    score_pass(jb0, 0, jb0 < i)

    def body(jb, carry):
        score_pass(jb, 1, True)
        value_pass(jb, 0)
        score_pass(jb + 1, 0, jb + 1 < i)
        value_pass(jb, 1)
        return carry

    lax.fori_loop(jb0, i, body, 0)
    score_pass(i, 1, False)
    value_pass(i, 0)
    value_pass(i, 1)
    outs = [acc_ref[h] / pltpu.roll(acc_ref[h], HEAD_DIM, 1) for h in range(2)]
    o = jnp.where(lm0, outs[0], outs[1]) * _silu(g_ref[...])
    o_ref[...] = o.astype(o_ref.dtype)


def _attn_prompt(jb0, kv, c, proj, *, t):
    bq = c.shape[-1]
    nq = t // bq
    return pl.pallas_call(
        functools.partial(_attn_prompt_kernel, bq=bq),
        grid_spec=pltpu.PrefetchScalarGridSpec(
            num_scalar_prefetch=1,
            grid=(N_PAIR, nq),
            in_specs=[pl.BlockSpec((bq, LANES), lambda p, i, tab: (i, COL_C // LANES + p)),
                      pl.BlockSpec((t, LANES), lambda p, i, tab: (0, p)),
                      pl.BlockSpec((t, LANES), lambda p, i, tab: (0, N_PAIR + p)),
                      pl.BlockSpec((1, nq, 2, bq), lambda p, i, tab: (p, 0, 0, 0)),
                      pl.BlockSpec((bq, LANES), lambda p, i, tab: (i, (COL_C + 3 * D_C) // LANES + p))],
            out_specs=pl.BlockSpec((bq, LANES), lambda p, i, tab: (i, p)),
            scratch_shapes=[pltpu.VMEM((2, bq, LANES), F32), pltpu.VMEM((2, bq, LANES), F32),
                            pltpu.VMEM((2, bq, LANES), F32), pltpu.VMEM((2, bq, bq), F32)]),
        out_shape=jax.ShapeDtypeStruct((t, D_C), BF16),
        compiler_params=pltpu.CompilerParams(
            dimension_semantics=("arbitrary", "arbitrary"), vmem_limit_bytes=VMEM_LIMIT),
        name="fox_attn_prompt",
    )(jb0, proj, kv, kv, c, proj)


def _attn_sample_kernel(q_ref, kn_ref, vn_ref, g_ref, kc_ref, vc_ref, c_ref, o_ref, *, past, ts):
    q = q_ref[...].astype(BF16)
    lm0 = lax.broadcasted_iota(jnp.int32, (ts, LANES), 1) < HEAD_DIM
    row = lax.broadcasted_iota(jnp.int32, (ts, ts), 0)
    col = lax.broadcasted_iota(jnp.int32, (ts, ts), 1)
    scale = jnp.asarray(HEAD_DIM ** -0.5, BF16)
    kc = kc_ref[0].astype(BF16)
    vc = vc_ref[0].astype(BF16)
    kn = kn_ref[...].astype(BF16)
    vn = vn_ref[...].astype(BF16)
    outs = []
    for h in range(2):
        qh = jnp.where(lm0 if h == 0 else jnp.logical_not(lm0), q, jnp.zeros_like(q)) * scale
        s1 = _dot_nt(qh, kc) - c_ref[0, 0, h:h + 1, 0:past]
        s2 = _dot_nt(qh, kn) - c_ref[0, 0, h:h + 1, past:past + ts]
        s2 = jnp.where(col <= row, s2, NEG)
        m = jnp.maximum(jnp.max(s1, axis=1, keepdims=True), jnp.max(s2, axis=1, keepdims=True))
        p1 = jnp.exp(s1 - m)
        p2 = jnp.exp(s2 - m)
        l = jnp.sum(p1, axis=1, keepdims=True) + jnp.sum(p2, axis=1, keepdims=True)
        outs.append((_dot(p1.astype(BF16), vc) + _dot(p2.astype(BF16), vn)) / l)
    o = jnp.where(lm0, outs[0], outs[1]) * _silu(g_ref[...])
    o_ref[...] = o.astype(o_ref.dtype)


def _attn_sample(proj, cache_k, cache_v, c, *, row0, layer, nseq, ts):
    past = cache_k.shape[1]
    blk0 = row0 // ts
    pcol = lambda base: pl.BlockSpec((ts, LANES), lambda s, p, base=base: (blk0 + s, base // LANES + p))
    cache = pl.BlockSpec((1, past, LANES), lambda s, p: (layer * nseq + s, 0, p))
    return pl.pallas_call(
        functools.partial(_attn_sample_kernel, past=past, ts=ts),
        grid=(nseq, N_PAIR),
        in_specs=[pcol(COL_C), pcol(COL_C + D_C), pcol(COL_C + 2 * D_C), pcol(COL_C + 3 * D_C),
                  cache, cache,
                  pl.BlockSpec((1, 1, 2, c.shape[-1]), lambda s, p: (s, p, 0, 0))],
        out_specs=pl.BlockSpec((ts, LANES), lambda s, p: (s, p)),
        out_shape=jax.ShapeDtypeStruct((nseq * ts, D_C), BF16),
        compiler_params=pltpu.CompilerParams(
            dimension_semantics=("arbitrary", "arbitrary"), vmem_limit_bytes=VMEM_LIMIT),
        name="fox_attn_sample",
    )(proj, proj, proj, proj, cache_k, cache_v, c)


def _outproj_kernel(ya_ref, yb_ref, yc_ref, h_ref, w_ref, g_ref, o_ref):
    mix = (_dot(ya_ref[...], w_ref[0:D_A, :])
           + _dot(yb_ref[...], w_ref[D_A:D_A + D_B, :])
           + _dot(yc_ref[...], w_ref[D_A + D_B:, :]))
    ms = jnp.mean(mix * mix, axis=-1, keepdims=True)
    o_ref[...] = h_ref[...] + (mix * lax.rsqrt(ms + RMS_EPS)) * g_ref[...]


def _outproj(ya, yb, yc, h, w_out, g, layer):
    m, d = h.shape
    tm = _tile(m, 512)
    rows = lambda w: pl.BlockSpec((tm, w), lambda i: (i, 0))
    return pl.pallas_call(
        _outproj_kernel,
        grid=(m // tm,),
        in_specs=[rows(D_A), rows(D_B), rows(D_C), rows(d),
                  pl.BlockSpec((None,) + w_out.shape[1:], lambda i: (layer, 0, 0)),
                  pl.BlockSpec((1, d), lambda i: (0, 0))],
        out_specs=rows(d),
        out_shape=jax.ShapeDtypeStruct((m, d), F32),
        compiler_params=pltpu.CompilerParams(
            dimension_semantics=("arbitrary",), vmem_limit_bytes=VMEM_LIMIT),
        name="outproj_postnorm",
    )(ya, yb, yc, h, w_out, g)


def _ple_kernel(h_ref, p_ref, wg_ref, wp_ref, o_ref):
    h = h_ref[...]
    gate = jax.nn.sigmoid(_dot(h.astype(BF16), wg_ref[...]))
    o_ref[...] = h + gate * _dot(p_ref[...].astype(BF16), wp_ref[...])


def _ple(h, p, w_gate, w_proj, layer):
    m, d = h.shape
    tm = _tile(m, 512)
    return pl.pallas_call(
        _ple_kernel,
        grid=(m // tm,),
        in_specs=[pl.BlockSpec((tm, d), lambda i: (i, 0)),
                  pl.BlockSpec((None, tm, D_PLE), lambda i: (layer, i, 0)),
                  pl.BlockSpec((None,) + w_gate.shape[1:], lambda i: (layer, 0, 0)),
                  pl.BlockSpec((None,) + w_proj.shape[1:], lambda i: (layer, 0, 0))],
        out_specs=pl.BlockSpec((tm, d), lambda i: (i, 0)),
        out_shape=jax.ShapeDtypeStruct((m, d), F32),
        compiler_params=pltpu.CompilerParams(
            dimension_semantics=("arbitrary",), vmem_limit_bytes=VMEM_LIMIT),
        name="ple_gate",
    )(h, p, w_gate, w_proj)


def kernel(x_prompt, x_sample, p_prompt, p_sample, cache_k, cache_v, cache_logf, state_conv, state_wkv, norm_pre_g, norm_post_g, w_in, b_f, conv_w, conv_b, conv_ln_g, conv_ln_b, rwkv_w0, rwkv_w1, rwkv_w2, rwkv_a0, rwkv_a1, rwkv_a2, rwkv_kk, rwkv_ka, rwkv_rk, rwkv_ln_g, rwkv_ln_b, w_out, ple_proj, ple_gate):
    depth = w_in.shape[0]
    bp, tp, d = x_prompt.shape
    bs, ts, _ = x_sample.shape
    past = cache_k.shape[2]
    n_p = bp * tp
    n_s = bs * ts
    assert tp % CHUNK == 0 and ts == CHUNK and past % CUM_W == 0 and tp % CUM_W == 0 and bp == 1
    bq = _tile(tp, ATTN_BLOCK)

    zpad = lambda n: jnp.zeros((depth, d, n), F32)
    w_ext = jnp.concatenate(
        [w_in, zpad(COL_LORA - w_in.shape[2]), rwkv_w1, rwkv_a1, zpad(N_EXT - COL_LORA - 2 * R_LORA)],
        axis=2).astype(BF16)
    zl = jnp.zeros((depth, R_LORA, D_B), F32)
    w_lora = jnp.concatenate([jnp.concatenate([rwkv_w2, zl], axis=2),
                              jnp.concatenate([zl, rwkv_a2], axis=2)], axis=1).astype(BF16)
    par = jnp.stack([rwkv_w0, rwkv_a0, rwkv_kk, rwkv_ka, rwkv_rk.reshape(depth, D_B), rwkv_ln_g, rwkv_ln_b,
                     jnp.zeros_like(rwkv_w0)], axis=1)
    w_out_b = w_out.astype(BF16)
    ple_gate_b = ple_gate.astype(BF16)
    ple_proj_b = ple_proj.astype(BF16)
    b_f16 = jnp.broadcast_to(jnp.pad(b_f, ((0, 0), (0, 16 - N_HEADS)))[:, :, None], (depth, 16, CUM_W))
    cache_k2 = cache_k.reshape(depth * bs, past, D_C).astype(BF16)
    cache_v2 = cache_v.reshape(depth * bs, past, D_C).astype(BF16)

    h = jnp.concatenate([x_prompt.reshape(n_p, d), x_sample.reshape(n_s, d)], axis=0)
    p_all = jnp.concatenate([p_prompt.reshape(depth, n_p, D_PLE), p_sample.reshape(depth, n_s, D_PLE)], axis=1)
    zero_conv = jnp.zeros((bp, K_CONV - 1, D_A), F32)
    zero_wkv = jnp.zeros((bp, N_PAIR, LANES, LANES), F32)

    outs = {name: [] for name in ("conv_p", "wkv_p", "k_p", "v_p", "lf_p", "conv_s", "wkv_s", "k_s", "v_s", "lf_s")}
    for l in range(depth):
        proj = _inproj(h, norm_pre_g[l][None], w_ext, l)

        ya_p, conv_p = _conv(proj, zero_conv, conv_w[l], conv_b[l][None], conv_ln_g[l][None],
                             conv_ln_b[l][None], row0=0, nseq=bp, t_seq=tp)
        ya_s, conv_s = _conv(proj, state_conv[l], conv_w[l], conv_b[l][None], conv_ln_g[l][None],
                             conv_ln_b[l][None], row0=n_p, nseq=bs, t_seq=ts)
        ya = jnp.concatenate([ya_p, ya_s], axis=0)

        yb_p, wkv_p = _rwkv(proj, zero_wkv, par[l], w_lora[l], row0=0, n_chunks=tp // CHUNK, chunk_is_seq=False)
        yb_s, wkv_s = _rwkv(proj, _pair_blockdiag(state_wkv[l]), par[l], w_lora[l],
                            row0=n_p, n_chunks=bs, chunk_is_seq=True)
        yb = jnp.concatenate([yb_p, yb_s], axis=0)

        k_new = proj[:, COL_C + D_C:COL_C + 2 * D_C]
        v_new = proj[:, COL_C + 2 * D_C:COL_C + 3 * D_C]
        f_raw = proj[:, COL_F:COL_F + N_HEADS]

        lf_pc, c_pc = _fox_prep(_to_time_chunks(f_raw[:n_p].reshape(bp, tp, N_HEADS)), b_f16[l], tp // CUM_W)
        lf_p = _from_time_chunks(lf_pc)
        c_p = c_pc[0].transpose(1, 0, 2).reshape(16, tp)[:N_HEADS]
        c_p = c_p.reshape(N_PAIR, 2, tp // bq, bq).transpose(0, 2, 1, 3)
        kv_p = proj[:n_p, COL_C + D_C:COL_C + 3 * D_C].astype(BF16)
        jb0 = _first_key_block(_attn_stats(proj, t=tp, bq=bq), c_p)
        yc_p = _attn_prompt(jb0, kv_p, c_p, proj, t=tp)

        x_s = jnp.concatenate(
            [_to_time_chunks(cache_logf[l]),
             _to_time_chunks(jnp.pad(f_raw[n_p:].reshape(bs, ts, N_HEADS), ((0, 0), (0, CUM_W - ts), (0, 0))))],
            axis=1)
        lf_sc, c_sc = _fox_prep(x_s, b_f16[l], 1)
        lf_s = _from_time_chunks(lf_sc[:, -1:])[:, :ts]
        c_s = c_sc.transpose(0, 2, 1, 3).reshape(bs, 16, past + CUM_W)[:, :N_HEADS]
        c_s = c_s.reshape(bs, N_PAIR, 2, past + CUM_W)
        yc_s = _attn_sample(proj, cache_k2, cache_v2, c_s, row0=n_p, layer=l, nseq=bs, ts=ts)
        yc = jnp.concatenate([yc_p, yc_s], axis=0)

        h = _outproj(ya, yb, yc, h, w_out_b, norm_post_g[l][None], l)
        h = _ple(h, p_all, ple_gate_b, ple_proj_b, l)

        outs["conv_p"].append(conv_p)
        outs["conv_s"].append(conv_s)
        outs["wkv_p"].append(_pair_unblock(wkv_p))
        outs["wkv_s"].append(_pair_unblock(wkv_s))
        outs["k_p"].append(k_new[:n_p].reshape(bp, tp, N_HEADS, HEAD_DIM))
        outs["v_p"].append(v_new[:n_p].reshape(bp, tp, N_HEADS, HEAD_DIM))
        outs["k_s"].append(k_new[n_p:].reshape(bs, ts, N_HEADS, HEAD_DIM))
        outs["v_s"].append(v_new[n_p:].reshape(bs, ts, N_HEADS, HEAD_DIM))
        outs["lf_p"].append(lf_p)
        outs["lf_s"].append(lf_s)

    st = lambda name: jnp.stack(outs[name])
    return (h[:n_p].reshape(bp, tp, d), h[n_p:].reshape(bs, ts, d),
            st("conv_p"), st("wkv_p"), st("k_p"), st("v_p"), st("lf_p"),
            st("conv_s"), st("wkv_s"), st("k_s"), st("v_s"), st("lf_s"))
```

```python
import functools

import jax
import jax.numpy as jnp
from jax import lax
from jax.experimental import pallas as pl
from jax.experimental.pallas import tpu as pltpu

F32 = jnp.float32
BF16 = jnp.bfloat16

D_MODEL = 2048
D_PLE = 256
HEAD_DIM = 64
D_A = 512
N_HEADS = 12
D_B = N_HEADS * HEAD_DIM
D_C = N_HEADS * HEAD_DIM
K_CONV = 31
R_LORA = 64
RMS_EPS = 1e-6
LN_EPS = 1e-5
GN_EPS = 64e-5

LANES = 128
N_PAIR = N_HEADS // 2
CHUNK = 64
CONV_HALO = 32
CUM_W = 128
ATTN_BLOCK = 512

COL_A = 0
COL_B = 3 * D_A
COL_C = COL_B + 4 * D_B
COL_F = COL_C + 4 * D_C
COL_LORA = COL_F + LANES
N_EXT = 8192
VMEM_LIMIT = 52 * 1024 * 1024


def _tile(n, pref):
    t = min(pref, n)
    while n % t:
        t //= 2
    return t


def _dot(a, b):
    return jnp.dot(a, b, preferred_element_type=F32)


def _dot_nt(a, b):
    return lax.dot_general(a, b, (((1,), (1,)), ((), ())), preferred_element_type=F32)


def _dot_tn(a, b):
    return lax.dot_general(a, b, (((0,), (0,)), ((), ())), preferred_element_type=F32)


def _split3(x):
    hi = x.astype(BF16)
    r1 = x - hi.astype(F32)
    mid = r1.astype(BF16)
    lo = (r1 - mid.astype(F32)).astype(BF16)
    return hi, mid, lo


def _dot_exact_r(x, m):
    hi, mid, lo = _split3(x)
    return _dot(hi, m) + _dot(mid, m) + _dot(lo, m)


def _dot_split_r(x, m):
    hi = x.astype(BF16)
    lo = (x - hi.astype(F32)).astype(BF16)
    return _dot(hi, m) + _dot(lo, m)


def _dot_split_l(m, x):
    hi = x.astype(BF16)
    lo = (x - hi.astype(F32)).astype(BF16)
    return _dot(m, hi) + _dot(m, lo)


def _softplus(z):
    return jnp.maximum(z, 0.0) + jnp.log(1.0 + jnp.exp(-jnp.abs(z)))


def _silu(z):
    return z * jax.nn.sigmoid(z)


def _inproj_kernel(x_ref, g_ref, w_ref, o_ref):
    x = x_ref[...]
    ms = jnp.mean(x * x, axis=-1, keepdims=True)
    xn = (x * lax.rsqrt(ms + RMS_EPS)) * g_ref[...]
    o_ref[...] = _dot(xn.astype(BF16), w_ref[...])


def _inproj(h, g, w_ext, layer):
    m, d = h.shape
    n = w_ext.shape[2]
    tm = _tile(m, 512)
    tn = 2048
    return pl.pallas_call(
        _inproj_kernel,
        grid=(n // tn, m // tm),
        in_specs=[pl.BlockSpec((tm, d), lambda j, i: (i, 0)),
                  pl.BlockSpec((1, d), lambda j, i: (0, 0)),
                  pl.BlockSpec((None, d, tn), lambda j, i: (layer, 0, j))],
        out_specs=pl.BlockSpec((tm, tn), lambda j, i: (i, j)),
        out_shape=jax.ShapeDtypeStruct((m, n), F32),
        compiler_params=pltpu.CompilerParams(
            dimension_semantics=("arbitrary", "arbitrary"), vmem_limit_bytes=VMEM_LIMIT),
        name="inproj",
    )(h, g, w_ext)


def _conv_kernel(av_ref, ag_ref, gate_ref, st_ref, w_ref, b_ref, lg_ref, lb_ref,
                 y_ref, sto_ref, ext_ref, *, tt, rb):
    t = pl.program_id(1)

    @pl.when(t == 0)
    def _init():
        ext_ref[0:8, :] = jnp.zeros((8, D_A), F32)
        ext_ref[CONV_HALO - (K_CONV - 1):CONV_HALO, :] = st_ref[0]

    ext_ref[CONV_HALO:CONV_HALO + tt, :] = av_ref[...] * jax.nn.sigmoid(ag_ref[...])
    off = CONV_HALO - (K_CONV - 1)

    for i in range(tt // rb):
        r0 = i * rb
        ys = []
        for c in range(D_A // LANES):
            cs = slice(LANES * c, LANES * (c + 1))
            acc = jnp.broadcast_to(b_ref[:, cs], (rb, LANES))
            for j in range(K_CONV):
                acc = acc + w_ref[j:j + 1, cs] * ext_ref[r0 + off + j:r0 + off + j + rb, cs]
            ys.append(acc)
        y = jnp.concatenate(ys, axis=1)
        mu = jnp.mean(y, axis=-1, keepdims=True)
        d = y - mu
        var = jnp.mean(d * d, axis=-1, keepdims=True)
        yn = d * lax.rsqrt(var + LN_EPS) * lg_ref[...] + lb_ref[...]
        out = _silu(yn) * _silu(gate_ref[r0:r0 + rb, :])
        y_ref[r0:r0 + rb, :] = out.astype(y_ref.dtype)
    sto_ref[0] = ext_ref[tt + off:tt + CONV_HALO, :]
    ext_ref[0:CONV_HALO, :] = ext_ref[tt:tt + CONV_HALO, :]


def _into(kernel, n_in, into):
    if into is None:
        return kernel, [], [], {}

    def wrapped(*refs):
        return kernel(*refs[:n_in], *refs[n_in + 1:])

    return wrapped, [pl.BlockSpec(memory_space=pl.ANY)], [into], {n_in: 0}


def _conv(proj, state, w, b, lg, lb, *, row0, nseq, t_seq, into=None):
    tt = _tile(t_seq, 256)
    rb = min(64, tt)
    nt = t_seq // tt
    blk0 = row0 // tt
    col = lambda c: pl.BlockSpec((tt, D_A), lambda s, t, c=c: (blk0 + s * nt + t, c))
    vec = lambda r: pl.BlockSpec((r, D_A), lambda s, t: (0, 0))
    body, extra_specs, extra_args, aliases = _into(functools.partial(_conv_kernel, tt=tt, rb=rb), 8, into)
    y, st = pl.pallas_call(
        body,
        grid=(nseq, nt),
        in_specs=[col(0), col(1), col(2),
                  pl.BlockSpec((1, K_CONV - 1, D_A), lambda s, t: (s, 0, 0)),
                  vec(K_CONV), vec(1), vec(1), vec(1)] + extra_specs,
        out_specs=[pl.BlockSpec((tt, D_A), lambda s, t: (blk0 + s * nt + t, 0)),
                   pl.BlockSpec((1, K_CONV - 1, D_A), lambda s, t: (s, 0, 0))],
        out_shape=[jax.ShapeDtypeStruct((proj.shape[0], D_A), BF16),
                   jax.ShapeDtypeStruct((nseq, K_CONV - 1, D_A), F32)],
        scratch_shapes=[pltpu.VMEM((tt + CONV_HALO, D_A), F32)],
        input_output_aliases=aliases,
        compiler_params=pltpu.CompilerParams(dimension_semantics=("arbitrary", "arbitrary")),
        name="conv_branch",
    )(proj, proj, proj, state, w, b, lg, lb, *extra_args)
    return y, st


def _rwkv_kernel(r_ref, k_ref, v_ref, g_ref, wa_ref, s0_ref, par_ref, wl_ref,
                 y_ref, so_ref, *scratch, nc, chunk_is_seq):
    rows = nc * CHUNK
    n2 = 2 * CHUNK
    dp = 2 * LANES
    ri = lax.broadcasted_iota(jnp.int32, (n2, n2), 0)
    ci = lax.broadcasted_iota(jnp.int32, (n2, n2), 1)
    same = (ri >= CHUNK) == (ci >= CHUNK)
    m_strict = same & (ci < ri)
    m_incl = same & (ci <= ri)
    eye = (ri == ci).astype(F32)
    rt = lax.broadcasted_iota(jnp.int32, (rows, rows), 0)
    ct = lax.broadcasted_iota(jnp.int32, (rows, rows), 1)
    tri = (((rt // CHUNK) == (ct // CHUNK)) & (ct <= rt)).astype(BF16)
    rh = lax.broadcasted_iota(jnp.int32, (dp, dp), 0)
    ch = lax.broadcasted_iota(jnp.int32, (dp, dp), 1)
    head_ones = ((rh // HEAD_DIM) == (ch // HEAD_DIM)).astype(BF16)
    lm0 = lax.broadcasted_iota(jnp.int32, (CHUNK, LANES), 1) < HEAD_DIM

    def stack(z):
        return jnp.concatenate([jnp.where(lm0, z, 0.0), jnp.where(lm0, 0.0, z)], axis=0).astype(BF16)

    def dup(z):
        zb = z.astype(BF16)
        return jnp.concatenate([zb, zb], axis=0)

    if not chunk_is_seq:
        s_ref = scratch[0]

        @pl.when(pl.program_id(0) == 0)
        def _init():
            s_ref[...] = s0_ref[0]

    wa = wa_ref[...]
    lane = lax.broadcasted_iota(jnp.int32, wa.shape, 1)
    lora = _dot(jnp.where(lane < R_LORA, jnp.tanh(wa), wa).astype(BF16), wl_ref[...])
    lw_all = -jnp.exp(-_softplus(-(par_ref[0:1, :] + lora[:, :D_B])) - 0.5)
    a_all = jax.nn.sigmoid(par_ref[1:2, :] + lora[:, D_B:])
    cl_all = _dot_split_l(tri, lw_all)

    chains = {}
    bonus_q = []
    for q in range(N_PAIR // 2):
        qs = slice(dp * q, dp * (q + 1))
        r = r_ref[:, qs]
        k = k_ref[:, qs]
        v = v_ref[:, qs]
        a = a_all[:, qs]
        lw = lw_all[:, qs]
        cl = cl_all[:, qs]
        kk = k * par_ref[2:3, qs]
        kh = k * (1.0 + (a - 1.0) * par_ref[3:4, qs])
        sums = _dot_split_r(jnp.concatenate([kk * kk, r * kh * par_ref[4:5, qs]], axis=0), head_ones)
        kk = kk / jnp.maximum(jnp.sqrt(sums[:rows]), 1e-12)
        bonus = sums[rows:] * v
        b = kk * a
        at = -(kk * jnp.exp(cl - lw))
        rt_ = r * jnp.exp(cl)
        p_inv = jnp.exp(-cl)
        kt = kh * p_inv
        bt = b * p_inv
        cl_last = jnp.concatenate(
            [jnp.broadcast_to(cl[CHUNK * (c + 1) - 1:CHUNK * (c + 1), :], (CHUNK, dp)) for c in range(nc)], axis=0)
        p_rem = jnp.exp(cl_last - cl)
        kp = kh * p_rem
        bp = b * p_rem
        p_end = jnp.exp(cl_last)

        bonus_q.append(bonus)
        for hh in range(2):
            ls_ = slice(LANES * hh, LANES * (hh + 1))
            for c in range(nc):
                rs_ = slice(CHUNK * c, CHUNK * (c + 1))
                ch_ = dict(
                    lr=jnp.concatenate([stack(at[rs_, ls_]), stack(rt_[rs_, ls_])], axis=0),
                    bk=jnp.concatenate([dup(bt[rs_, ls_]), dup(kt[rs_, ls_])], axis=0),
                    vs=stack(v[rs_, ls_]),
                    kbp=jnp.concatenate([stack(kp[rs_, ls_]), stack(bp[rs_, ls_])], axis=0),
                    pe=p_end[CHUNK * c:CHUNK * c + 1, ls_])
                chains[(2 * q + hh, c)] = ch_

    order = [(p, c) for c in range(nc) for p in range(N_PAIR)]
    for key in order:
        ch_ = chains[key]
        gm = _dot_nt(ch_["lr"], ch_.pop("bk"))
        x = jnp.where(m_strict, gm[:n2, :n2], 0.0)
        ch_["a_k"] = jnp.concatenate([jnp.where(m_strict, gm[:n2, n2:], 0.0),
                                      jnp.where(m_incl, gm[n2:, n2:], 0.0)], axis=0).astype(BF16)
        ch_["a_rb"] = jnp.where(m_incl, gm[n2:, :n2], 0.0).astype(BF16)
        ch_["tinv"] = eye + x
        ch_["xb"] = x.astype(BF16)
    for key in order:
        ch_ = chains[key]
        xb = ch_.pop("xb")
        ch_["xx"] = _dot(xb, xb).astype(BF16)
        ch_["av"] = _dot(ch_.pop("a_k"), ch_["vs"])
    for _ in range(4):
        for key in order:
            ch_ = chains[key]
            both = _dot(ch_["xx"], jnp.concatenate([ch_["tinv"].astype(BF16), ch_["xx"]], axis=1))
            ch_["tinv"] = ch_["tinv"] + both[:, :n2]
            ch_["xx"] = both[:, n2:].astype(BF16)
    for key in order:
        ch_ = chains[key]
        ch_["tinv"] = (ch_["tinv"] + _dot(ch_.pop("xx"), ch_["tinv"].astype(BF16))).astype(BF16)

    states = [None if chunk_is_seq else s_ref[p] for p in range(N_PAIR)]
    ys = {}
    for c in range(nc):
        if chunk_is_seq:
            states = [s0_ref[c, p] for p in range(N_PAIR)]
        sls = [_dot_nt(chains[(p, c)]["lr"], states[p].astype(BF16)) for p in range(N_PAIR)]
        uss = [_dot(chains[(p, c)]["tinv"], (sls[p][:n2] + chains[(p, c)]["av"][:n2]).astype(BF16)).astype(BF16)
               for p in range(N_PAIR)]
        for p in range(N_PAIR):
            ch_ = chains[(p, c)]
            yst = sls[p][n2:] + ch_["av"][n2:] + _dot(ch_["a_rb"], uss[p])
            ys[(p, c)] = yst[:CHUNK] + yst[CHUNK:]
            states[p] = states[p] * ch_["pe"] + _dot_tn(jnp.concatenate([ch_["vs"], uss[p]], axis=0), ch_["kbp"])
            if chunk_is_seq:
                so_ref[c, p] = states[p]
    if not chunk_is_seq:
        for p in range(N_PAIR):
            s_ref[p] = states[p]
            so_ref[0, p] = states[p]

    for q in range(N_PAIR // 2):
        qs = slice(dp * q, dp * (q + 1))
        bonus = bonus_q[q]
        y = jnp.concatenate([jnp.concatenate([ys[(2 * q + hh, c)] for c in range(nc)], axis=0)
                             for hh in range(2)], axis=1)
        mu = _dot_split_r(y, head_ones) * (1.0 / HEAD_DIM)
        d = y - mu
        var = _dot_split_r(d * d, head_ones) * (1.0 / HEAD_DIM)
        yn = d * lax.rsqrt(var + GN_EPS) * par_ref[5:6, qs] + par_ref[6:7, qs]
        y_ref[:, qs] = ((yn + bonus) * _silu(g_ref[:, qs])).astype(y_ref.dtype)


def _rwkv(proj, s0, par, wl, *, row0, n_chunks, chunk_is_seq, into=None):
    nc = _tile(n_chunks, 4)
    rows = nc * CHUNK
    blk0 = row0 // rows
    colb = lambda c: pl.BlockSpec((rows, D_B), lambda i, c=c: (blk0 + i, COL_B // D_B + c))
    full = lambda a: pl.BlockSpec(a.shape, lambda i: (0,) * a.ndim)
    sblk = nc if chunk_is_seq else 1
    smap = (lambda i: (i, 0, 0, 0)) if chunk_is_seq else (lambda i: (0, 0, 0, 0))
    state = pl.BlockSpec((sblk, N_PAIR, LANES, LANES), smap)
    body, extra_specs, extra_args, aliases = _into(
        functools.partial(_rwkv_kernel, nc=nc, chunk_is_seq=chunk_is_seq), 8, into)
    return pl.pallas_call(
        body,
        grid=(n_chunks // nc,),
        in_specs=[colb(0), colb(1), colb(2), colb(3),
                  pl.BlockSpec((rows, LANES), lambda i: (blk0 + i, COL_LORA // LANES)),
                  state, full(par), full(wl)] + extra_specs,
        out_specs=[pl.BlockSpec((rows, D_B), lambda i: (blk0 + i, 0)), state],
        out_shape=[jax.ShapeDtypeStruct((proj.shape[0], D_B), BF16),
                   jax.ShapeDtypeStruct(s0.shape, F32)],
        scratch_shapes=[] if chunk_is_seq else [pltpu.VMEM((N_PAIR, LANES, LANES), F32)],
        input_output_aliases=aliases,
        compiler_params=pltpu.CompilerParams(dimension_semantics=("arbitrary",), vmem_limit_bytes=VMEM_LIMIT),
        name="rwkv7_seqchunks" if chunk_is_seq else "rwkv7_stream",
    )(proj, proj, proj, proj, proj, s0, par, wl, *extra_args)


def _pair_blockdiag(s):
    b = s.shape[0]
    s = s.reshape(b, N_PAIR, 2, HEAD_DIM, HEAD_DIM)
    z = jnp.zeros_like(s[:, :, 0])
    top = jnp.concatenate([s[:, :, 0], z], axis=-1)
    bot = jnp.concatenate([z, s[:, :, 1]], axis=-1)
    return jnp.concatenate([top, bot], axis=-2)


def _pair_unblock(sp):
    b = sp.shape[0]
    h0 = sp[:, :, :HEAD_DIM, :HEAD_DIM]
    h1 = sp[:, :, HEAD_DIM:, HEAD_DIM:]
    return jnp.stack([h0, h1], axis=2).reshape(b, N_HEADS, HEAD_DIM, HEAD_DIM)


def _fox_prep_kernel(x_ref, b_ref, lf_ref, c_ref, carry_ref, *, n_chunks, n_raw):
    rr = lax.broadcasted_iota(jnp.int32, (CUM_W, CUM_W), 0)
    cc = lax.broadcasted_iota(jnp.int32, (CUM_W, CUM_W), 1)
    upper = (rr <= cc).astype(BF16)
    ones = jnp.ones((CUM_W, CUM_W), BF16)
    carry_ref[...] = jnp.zeros_like(carry_ref)

    nh = x_ref.shape[2]

    def run(start, count, raw):
        if count == 0:
            return
        g = _tile(count, 8)

        def body(gi, carry):
            c0 = start + gi * g
            x = x_ref[0, pl.ds(c0, g)]
            if raw:
                x = -_softplus(-(x + b_ref[...][None]))
            lf_ref[0, pl.ds(c0, g)] = x
            x2 = x.reshape(g * nh, CUM_W)
            within = _dot_exact_r(x2, upper)
            total = _dot_exact_r(x2, ones)
            run_carry = carry_ref[...]
            for k in range(g):
                c_ref[0, c0 + k] = within[k * nh:(k + 1) * nh] + run_carry
                run_carry = run_carry + total[k * nh:(k + 1) * nh]
            carry_ref[...] = run_carry
            return carry

        lax.fori_loop(0, count // g, body, 0)

    run(0, n_chunks - n_raw, False)
    run(n_chunks - n_raw, n_raw, True)


def _fox_prep(x, b, n_raw):
    nb, nch = x.shape[:2]
    spec = pl.BlockSpec((1,) + x.shape[1:], lambda s: (s, 0, 0, 0))
    return pl.pallas_call(
        functools.partial(_fox_prep_kernel, n_chunks=nch, n_raw=n_raw),
        grid=(nb,),
        in_specs=[spec, pl.BlockSpec(b.shape, lambda s: (0, 0))],
        out_specs=[spec, spec],
        out_shape=[jax.ShapeDtypeStruct(x.shape, F32)] * 2,
        scratch_shapes=[pltpu.VMEM(x.shape[2:], F32)],
        compiler_params=pltpu.CompilerParams(dimension_semantics=("arbitrary",)),
        name="fox_logf_cumsum",
    )(x, b)


def _to_time_chunks(x):
    b, t, hh = x.shape
    x = jnp.pad(x, ((0, 0), (0, 0), (0, 16 - hh)))
    return x.reshape(b, t // CUM_W, CUM_W, 16).transpose(0, 1, 3, 2)


def _from_time_chunks(x):
    b, nch = x.shape[:2]
    return x.transpose(0, 1, 3, 2).reshape(b, nch * CUM_W, 16)[:, :, :N_HEADS]


NEG = -1e30
LOG2E = 1.4426950408889634
PRUNE_LOG2 = 160.0


def _attn_stats_kernel(q_ref, k_ref, o_ref):
    n = q_ref.shape[0]
    dp = 2 * LANES
    rr = lax.broadcasted_iota(jnp.int32, (dp, dp), 0)
    cc = lax.broadcasted_iota(jnp.int32, (dp, dp), 1)
    head_ones = ((rr // HEAD_DIM) == (cc // HEAD_DIM)).astype(BF16)
    for j in range(D_C // dp):
        js = slice(dp * j, dp * (j + 1))
        q = q_ref[:, js]
        k = k_ref[:, js]
        sums = _dot_split_r(jnp.concatenate([q * q, k * k, q * k], axis=0), head_ones)
        qmax = jnp.sqrt(jnp.max(sums[:n], axis=0, keepdims=True))
        kmax = jnp.sqrt(jnp.max(sums[n:2 * n], axis=0, keepdims=True))
        dneg = jnp.max(-sums[2 * n:], axis=0, keepdims=True)
        o_ref[0, :, js] = jnp.concatenate([qmax, kmax, dneg, jnp.zeros((5, dp), F32)], axis=0)


def _attn_stats(proj, *, t, bq):
    nq = t // bq
    return pl.pallas_call(
        _attn_stats_kernel,
        grid=(nq,),
        in_specs=[pl.BlockSpec((bq, D_C), lambda i: (i, COL_C // D_C)),
                  pl.BlockSpec((bq, D_C), lambda i: (i, COL_C // D_C + 1))],
        out_specs=pl.BlockSpec((1, 8, D_C), lambda i: (i, 0, 0)),
        out_shape=jax.ShapeDtypeStruct((nq, 8, D_C), F32),
        compiler_params=pltpu.CompilerParams(dimension_semantics=("arbitrary",)),
        name="fox_attn_stats",
    )(proj, proj)


def _first_key_block(stats, c):
    nq = c.shape[1]
    sc = HEAD_DIM ** -0.5 * LOG2E
    per_head = lambda r: stats[:, r, ::HEAD_DIM].reshape(nq, N_PAIR, 2).transpose(1, 0, 2)
    qmax = per_head(0)
    kpre = lax.cummax(per_head(1), axis=1)
    dneg = per_head(2) * sc
    c2s = c[:, :, :, 0] * LOG2E
    c2e = c[:, :, :, -1] * LOG2E
    bound = (1.04 * sc * qmax[:, :, None, :] * kpre[:, None, :, :] + dneg[:, :, None, :]
             + c2s[:, :, None, :] - c2e[:, None, :, :])
    before = jnp.arange(nq)[None, :, None] > jnp.arange(nq)[None, None, :]
    skip = jnp.all(bound < -PRUNE_LOG2, axis=-1) & before
    return jnp.sum(jnp.cumprod(skip.astype(jnp.int32), axis=2), axis=2).astype(jnp.int32).reshape(-1)


def _attn_prompt_kernel(jb0_ref, q_ref, k_ref, v_ref, c_ref, g_ref, o_ref, m_ref, a_ref, acc_ref, s_ref, *, bq):
    i = pl.program_id(1)
    q = q_ref[...] * (HEAD_DIM ** -0.5 * LOG2E)
    lm0 = lax.broadcasted_iota(jnp.int32, (bq, LANES), 1) < HEAD_DIM
    own = (lm0, jnp.logical_not(lm0))
    row = lax.broadcasted_iota(jnp.int32, (bq, bq), 0)
    col = lax.broadcasted_iota(jnp.int32, (bq, bq), 1)
    causal = col <= row
    qs = (jnp.where(lm0, q, 0.0).astype(BF16), jnp.where(lm0, 0.0, q).astype(BF16))
    one = jnp.ones((bq, LANES), BF16)
    m_ref[...] = jnp.full(m_ref.shape, NEG, F32)
    acc_ref[...] = jnp.zeros(acc_ref.shape, F32)

    def score_pass(jb, h, visible):
        r0 = pl.multiple_of(jb * bq, bq)
        s = _dot_nt(qs[h], k_ref[pl.ds(r0, bq), :]) - c_ref[0, jb, h:h + 1, :] * LOG2E
        if visible is not True:
            s = jnp.where(causal if visible is False else jnp.logical_or(causal, visible), s, NEG)
        s_ref[h] = s
        m_old = m_ref[h]
        m_new = jnp.maximum(m_old, jnp.max(s, axis=1, keepdims=True))
        a_ref[h] = jnp.exp2(m_old - m_new)
        m_ref[h] = m_new

    def value_pass(jb, h):
        r0 = pl.multiple_of(jb * bq, bq)
        vh = jnp.where(own[h], v_ref[pl.ds(r0, bq), :], one)
        pr = jnp.exp2(s_ref[h] - jnp.concatenate([m_ref[h]] * (bq // LANES), axis=1))
        acc_ref[h] = a_ref[h] * acc_ref[h] + _dot(pr.astype(BF16), vh)

    jb0 = jb0_ref[pl.program_id(0) * pl.num_programs(1) + i]
    score_pass(jb0, 0, jb0 < i)

    def body(jb, carry):
        score_pass(jb, 1, True)
        value_pass(jb, 0)
        score_pass(jb + 1, 0, jb + 1 < i)
        value_pass(jb, 1)
        return carry

    lax.fori_loop(jb0, i, body, 0)
    score_pass(i, 1, False)
    value_pass(i, 0)
    value_pass(i, 1)
    outs = [acc_ref[h] / pltpu.roll(acc_ref[h], HEAD_DIM, 1) for h in range(2)]
    o = jnp.where(lm0, outs[0], outs[1]) * _silu(g_ref[...])
    o_ref[...] = o.astype(o_ref.dtype)


def _attn_prompt(jb0, kv, c, proj, *, t):
    bq = c.shape[-1]
    nq = t // bq
    return pl.pallas_call(
        functools.partial(_attn_prompt_kernel, bq=bq),
        grid_spec=pltpu.PrefetchScalarGridSpec(
            num_scalar_prefetch=1,
            grid=(N_PAIR, nq),
            in_specs=[pl.BlockSpec((bq, LANES), lambda p, i, tab: (i, COL_C // LANES + p)),
                      pl.BlockSpec((t, LANES), lambda p, i, tab: (0, p)),
                      pl.BlockSpec((t, LANES), lambda p, i, tab: (0, N_PAIR + p)),
                      pl.BlockSpec((1, nq, 2, bq), lambda p, i, tab: (p, 0, 0, 0)),
                      pl.BlockSpec((bq, LANES), lambda p, i, tab: (i, (COL_C + 3 * D_C) // LANES + p))],
            out_specs=pl.BlockSpec((bq, LANES), lambda p, i, tab: (i, p)),
            scratch_shapes=[pltpu.VMEM((2, bq, LANES), F32), pltpu.VMEM((2, bq, LANES), F32),
                            pltpu.VMEM((2, bq, LANES), F32), pltpu.VMEM((2, bq, bq), F32)]),
        out_shape=jax.ShapeDtypeStruct((proj.shape[0], D_C), BF16),
        compiler_params=pltpu.CompilerParams(
            dimension_semantics=("arbitrary", "arbitrary"), vmem_limit_bytes=VMEM_LIMIT),
        name="fox_attn_prompt",
    )(jb0, proj, kv, kv, c, proj)


def _attn_sample_kernel(q_ref, kn_ref, vn_ref, g_ref, kc_ref, vc_ref, c_ref, o_ref, *, past, ts):
    q = q_ref[...].astype(BF16)
    lm0 = lax.broadcasted_iota(jnp.int32, (ts, LANES), 1) < HEAD_DIM
    row = lax.broadcasted_iota(jnp.int32, (ts, ts), 0)
    col = lax.broadcasted_iota(jnp.int32, (ts, ts), 1)
    scale = jnp.asarray(HEAD_DIM ** -0.5, BF16)
    kc = kc_ref[0].astype(BF16)
    vc = vc_ref[0].astype(BF16)
    kn = kn_ref[...].astype(BF16)
    vn = vn_ref[...].astype(BF16)
    outs = []
    for h in range(2):
        qh = jnp.where(lm0 if h == 0 else jnp.logical_not(lm0), q, jnp.zeros_like(q)) * scale
        s1 = _dot_nt(qh, kc) - c_ref[0, 0, h:h + 1, 0:past]
        s2 = _dot_nt(qh, kn) - c_ref[0, 0, h:h + 1, past:past + ts]
        s2 = jnp.where(col <= row, s2, NEG)
        m = jnp.maximum(jnp.max(s1, axis=1, keepdims=True), jnp.max(s2, axis=1, keepdims=True))
        p1 = jnp.exp(s1 - m)
        p2 = jnp.exp(s2 - m)
        l = jnp.sum(p1, axis=1, keepdims=True) + jnp.sum(p2, axis=1, keepdims=True)
        outs.append((_dot(p1.astype(BF16), vc) + _dot(p2.astype(BF16), vn)) / l)
    o = jnp.where(lm0, outs[0], outs[1]) * _silu(g_ref[...])
    o_ref[...] = o.astype(o_ref.dtype)


def _attn_sample(proj, cache_k, cache_v, c, *, row0, layer, nseq, ts, into=None):
    past = cache_k.shape[1]
    blk0 = row0 // ts
    pcol = lambda base: pl.BlockSpec((ts, LANES), lambda s, p, base=base: (blk0 + s, base // LANES + p))
    cache = pl.BlockSpec((1, past, LANES), lambda s, p: (layer * nseq + s, 0, p))
    body, extra_specs, extra_args, aliases = _into(
        functools.partial(_attn_sample_kernel, past=past, ts=ts), 7, into)
    return pl.pallas_call(
        body,
        grid=(nseq, N_PAIR),
        in_specs=[pcol(COL_C), pcol(COL_C + D_C), pcol(COL_C + 2 * D_C), pcol(COL_C + 3 * D_C),
                  cache, cache,
                  pl.BlockSpec((1, 1, 2, c.shape[-1]), lambda s, p: (s, p, 0, 0))] + extra_specs,
        out_specs=pl.BlockSpec((ts, LANES), lambda s, p: (blk0 + s, p)),
        out_shape=jax.ShapeDtypeStruct((proj.shape[0], D_C), BF16),
        input_output_aliases=aliases,
        compiler_params=pltpu.CompilerParams(
            dimension_semantics=("arbitrary", "arbitrary"), vmem_limit_bytes=VMEM_LIMIT),
        name="fox_attn_sample",
    )(proj, proj, proj, proj, cache_k, cache_v, c, *extra_args)


def _outproj_kernel(ya_ref, yb_ref, yc_ref, h_ref, w_ref, g_ref, o_ref):
    mix = (_dot(ya_ref[...], w_ref[0:D_A, :])
           + _dot(yb_ref[...], w_ref[D_A:D_A + D_B, :])
           + _dot(yc_ref[...], w_ref[D_A + D_B:, :]))
    ms = jnp.mean(mix * mix, axis=-1, keepdims=True)
    o_ref[...] = h_ref[...] + (mix * lax.rsqrt(ms + RMS_EPS)) * g_ref[...]


def _outproj(ya, yb, yc, h, w_out, g, layer):
    m, d = h.shape
    tm = _tile(m, 512)
    rows = lambda w: pl.BlockSpec((tm, w), lambda i: (i, 0))
    return pl.pallas_call(
        _outproj_kernel,
        grid=(m // tm,),
        in_specs=[rows(D_A), rows(D_B), rows(D_C), rows(d),
                  pl.BlockSpec((None,) + w_out.shape[1:], lambda i: (layer, 0, 0)),
                  pl.BlockSpec((1, d), lambda i: (0, 0))],
        out_specs=rows(d),
        out_shape=jax.ShapeDtypeStruct((m, d), F32),
        compiler_params=pltpu.CompilerParams(
            dimension_semantics=("arbitrary",), vmem_limit_bytes=VMEM_LIMIT),
        name="outproj_postnorm",
    )(ya, yb, yc, h, w_out, g)


def _ple_kernel(h_ref, p_ref, wg_ref, wp_ref, o_ref):
    h = h_ref[...]
    gate = jax.nn.sigmoid(_dot(h.astype(BF16), wg_ref[...]))
    o_ref[...] = h + gate * _dot(p_ref[...].astype(BF16), wp_ref[...])


def _ple(h, p, w_gate, w_proj, layer):
    m, d = h.shape
    tm = _tile(m, 512)
    return pl.pallas_call(
        _ple_kernel,
        grid=(m // tm,),
        in_specs=[pl.BlockSpec((tm, d), lambda i: (i, 0)),
                  pl.BlockSpec((None, tm, D_PLE), lambda i: (layer, i, 0)),
                  pl.BlockSpec((None,) + w_gate.shape[1:], lambda i: (layer, 0, 0)),
                  pl.BlockSpec((None,) + w_proj.shape[1:], lambda i: (layer, 0, 0))],
        out_specs=pl.BlockSpec((tm, d), lambda i: (i, 0)),
        out_shape=jax.ShapeDtypeStruct((m, d), F32),
        compiler_params=pltpu.CompilerParams(
            dimension_semantics=("arbitrary",), vmem_limit_bytes=VMEM_LIMIT),
        name="ple_gate",
    )(h, p, w_gate, w_proj)


def kernel(x_prompt, x_sample, p_prompt, p_sample, cache_k, cache_v, cache_logf, state_conv, state_wkv, norm_pre_g, norm_post_g, w_in, b_f, conv_w, conv_b, conv_ln_g, conv_ln_b, rwkv_w0, rwkv_w1, rwkv_w2, rwkv_a0, rwkv_a1, rwkv_a2, rwkv_kk, rwkv_ka, rwkv_rk, rwkv_ln_g, rwkv_ln_b, w_out, ple_proj, ple_gate):
    depth = w_in.shape[0]
    bp, tp, d = x_prompt.shape
    bs, ts, _ = x_sample.shape
    past = cache_k.shape[2]
    n_p = bp * tp
    n_s = bs * ts
    assert tp % CHUNK == 0 and ts == CHUNK and past % CUM_W == 0 and tp % CUM_W == 0 and bp == 1
    bq = _tile(tp, ATTN_BLOCK)

    zpad = lambda n: jnp.zeros((depth, d, n), F32)
    w_ext = jnp.concatenate(
        [w_in, zpad(COL_LORA - w_in.shape[2]), rwkv_w1, rwkv_a1, zpad(N_EXT - COL_LORA - 2 * R_LORA)],
        axis=2).astype(BF16)
    zl = jnp.zeros((depth, R_LORA, D_B), F32)
    w_lora = jnp.concatenate([jnp.concatenate([rwkv_w2, zl], axis=2),
                              jnp.concatenate([zl, rwkv_a2], axis=2)], axis=1).astype(BF16)
    par = jnp.stack([rwkv_w0, rwkv_a0, rwkv_kk, rwkv_ka, rwkv_rk.reshape(depth, D_B), rwkv_ln_g, rwkv_ln_b,
                     jnp.zeros_like(rwkv_w0)], axis=1)
    w_out_b = w_out.astype(BF16)
    ple_gate_b = ple_gate.astype(BF16)
    ple_proj_b = ple_proj.astype(BF16)
    b_f16 = jnp.broadcast_to(jnp.pad(b_f, ((0, 0), (0, 16 - N_HEADS)))[:, :, None], (depth, 16, CUM_W))
    cache_k2 = cache_k.reshape(depth * bs, past, D_C)
    cache_v2 = cache_v.reshape(depth * bs, past, D_C)

    h = jnp.concatenate([x_prompt.reshape(n_p, d), x_sample.reshape(n_s, d)], axis=0)
    p_all = jnp.concatenate([p_prompt.reshape(depth, n_p, D_PLE), p_sample.reshape(depth, n_s, D_PLE)], axis=1)
    zero_conv = jnp.zeros((bp, K_CONV - 1, D_A), F32)
    zero_wkv = jnp.zeros((bp, N_PAIR, LANES, LANES), F32)

    outs = {name: [] for name in ("conv_p", "wkv_p", "k_p", "v_p", "lf_p", "conv_s", "wkv_s", "k_s", "v_s", "lf_s")}
    for l in range(depth):
        proj = _inproj(h, norm_pre_g[l][None], w_ext, l)

        ya, conv_p = _conv(proj, zero_conv, conv_w[l], conv_b[l][None], conv_ln_g[l][None],
                           conv_ln_b[l][None], row0=0, nseq=bp, t_seq=tp)
        ya, conv_s = _conv(proj, state_conv[l], conv_w[l], conv_b[l][None], conv_ln_g[l][None],
                           conv_ln_b[l][None], row0=n_p, nseq=bs, t_seq=ts, into=ya)

        yb, wkv_p = _rwkv(proj, zero_wkv, par[l], w_lora[l], row0=0, n_chunks=tp // CHUNK, chunk_is_seq=False)
        yb, wkv_s = _rwkv(proj, _pair_blockdiag(state_wkv[l]), par[l], w_lora[l],
                          row0=n_p, n_chunks=bs, chunk_is_seq=True, into=yb)

        k_new = proj[:, COL_C + D_C:COL_C + 2 * D_C]
        v_new = proj[:, COL_C + 2 * D_C:COL_C + 3 * D_C]
        f_raw = proj[:, COL_F:COL_F + N_HEADS]

        lf_pc, c_pc = _fox_prep(_to_time_chunks(f_raw[:n_p].reshape(bp, tp, N_HEADS)), b_f16[l], tp // CUM_W)
        lf_p = _from_time_chunks(lf_pc)
        c_p = c_pc[0].transpose(1, 0, 2).reshape(16, tp)[:N_HEADS]
        c_p = c_p.reshape(N_PAIR, 2, tp // bq, bq).transpose(0, 2, 1, 3)
        kv_p = proj[:n_p, COL_C + D_C:COL_C + 3 * D_C].astype(BF16)
        jb0 = _first_key_block(_attn_stats(proj, t=tp, bq=bq), c_p)
        yc_p = _attn_prompt(jb0, kv_p, c_p, proj, t=tp)

        x_s = jnp.concatenate(
            [_to_time_chunks(cache_logf[l]),
             _to_time_chunks(jnp.pad(f_raw[n_p:].reshape(bs, ts, N_HEADS), ((0, 0), (0, CUM_W - ts), (0, 0))))],
            axis=1)
        lf_sc, c_sc = _fox_prep(x_s, b_f16[l], 1)
        lf_s = _from_time_chunks(lf_sc[:, -1:])[:, :ts]
        c_s = c_sc.transpose(0, 2, 1, 3).reshape(bs, 16, past + CUM_W)[:, :N_HEADS]
        c_s = c_s.reshape(bs, N_PAIR, 2, past + CUM_W)
        yc = _attn_sample(proj, cache_k2, cache_v2, c_s, row0=n_p, layer=l, nseq=bs, ts=ts, into=yc_p)

        h = _outproj(ya, yb, yc, h, w_out_b, norm_post_g[l][None], l)
        h = _ple(h, p_all, ple_gate_b, ple_proj_b, l)

        outs["conv_p"].append(conv_p)
        outs["conv_s"].append(conv_s)
        outs["wkv_p"].append(_pair_unblock(wkv_p))
        outs["wkv_s"].append(_pair_unblock(wkv_s))
        outs["k_p"].append(k_new[:n_p].reshape(bp, tp, N_HEADS, HEAD_DIM))
        outs["v_p"].append(v_new[:n_p].reshape(bp, tp, N_HEADS, HEAD_DIM))
        outs["k_s"].append(k_new[n_p:].reshape(bs, ts, N_HEADS, HEAD_DIM))
        outs["v_s"].append(v_new[n_p:].reshape(bs, ts, N_HEADS, HEAD_DIM))
        outs["lf_p"].append(lf_p)
        outs["lf_s"].append(lf_s)

    st = lambda name: jnp.stack(outs[name])
    return (h[:n_p].reshape(bp, tp, d), h[n_p:].reshape(bs, ts, d),
            st("conv_p"), st("wkv_p"), st("k_p"), st("v_p"), st("lf_p"),
            st("conv_s"), st("wkv_s"), st("k_s"), st("v_s"), st("lf_s"))
```
